```python
import math
import jax
import jax.numpy as jnp
from jax import lax
import numpy as np

D_MODEL = 2048
BATCH = 8
SEQ = 2048
DEPTH = 2

GRID_W = 64
CTX_LEN = 256
EPS = 1e-6
N_MOD = 9
FFN_RESIDUAL = 0.5
D_FF = 5632

ATT_HEADS = 8
ATT_DH = 64
ATT_VD = 2 * ATT_DH
ATT_WIDTH = ATT_HEADS * ATT_VD
Q_BLOCK = 128
ROPE_BASE = 10000.0

SSM_INNER = 2048
SSM_HEADDIM = 64
SSM_HEADS = SSM_INNER // SSM_HEADDIM
SSM_GROUPS = 4
SSM_HPG = SSM_HEADS // SSM_GROUPS
SSM_STATE = 128
SSM_CHUNK = 128
CONV_W = 5
CONV_DIM = SSM_INNER + 2 * SSM_GROUPS * SSM_STATE
SSM_IN_COLS = SSM_INNER + CONV_DIM + 2 * SSM_HEADS

N_BRANCH = 2
IN_SPLITS = (ATT_WIDTH, 2 * ATT_WIDTH, 3 * ATT_WIDTH, 3 * ATT_WIDTH + SSM_IN_COLS)
IN_COLS = 3 * ATT_WIDTH + SSM_IN_COLS + N_BRANCH * D_MODEL

kernel_name = "hybrid_diffattn_ssd_dit_block"


def rms_norm(x, w):
    xf = x.astype(jnp.float32)
    y = xf * lax.rsqrt(jnp.mean(xf * xf, axis=-1, keepdims=True) + EPS)
    return (y * w.astype(jnp.float32)).astype(x.dtype)


def adaln_params(cond, w, b):
    m = jax.nn.silu(cond) @ w + b
    return m.reshape(cond.shape[0], 1, N_MOD, D_MODEL)


def modulated_norm(x, norm_w, mod, slot):
    shift, scale = mod[:, :, 3 * slot], mod[:, :, 3 * slot + 1]
    return rms_norm(x, norm_w) * (1.0 + scale) + shift


def swiglu(u, w_gu, w_down):
    g, v = jnp.split(u @ w_gu, 2, axis=-1)
    return (jax.nn.silu(g) * v) @ w_down


def ffn_sublayer(x, mod, slot, norm_w, w_gu, w_down):
    gate = mod[:, :, 3 * slot + 2]
    return x + FFN_RESIDUAL * gate * swiglu(modulated_norm(x, norm_w, mod, slot), w_gu, w_down)


def axial_rope(rows):
    n_freq = ATT_DH // 4
    inv = ROPE_BASE ** (-jnp.arange(n_freq, dtype=jnp.float32) / n_freq)
    row = jnp.repeat(jnp.arange(rows, dtype=jnp.float32), GRID_W)
    col = jnp.tile(jnp.arange(GRID_W, dtype=jnp.float32), rows)
    ang = jnp.concatenate([row[:, None] * inv, col[:, None] * inv], axis=-1)
    return jnp.cos(ang), jnp.sin(ang)


def apply_rope(t, cos, sin):
    tp = t.astype(jnp.float32).reshape(*t.shape[:-1], ATT_DH // 2, 2)
    cs = cos[None, :, None, None, :]
    sn = sin[None, :, None, None, :]
    t0, t1 = tp[..., 0], tp[..., 1]
    out = jnp.stack([t0 * cs - t1 * sn, t0 * sn + t1 * cs], axis=-1)
    return out.reshape(t.shape).astype(t.dtype)


def diff_attend(q, k, v, lam):
    s = jnp.einsum("bqhcd,bkhcd->bhcqk", q, k).astype(jnp.float32) * (ATT_DH ** -0.5)
    p = jax.nn.softmax(s, axis=-1)
    a = p[:, :, 0] - lam * p[:, :, 1]
    return jnp.einsum("bhqk,bkhe->bqhe", a.astype(v.dtype), v)


def latent_diff_attention(q_lat, k_all, v_all, lam):
    b, l = q_lat.shape[:2]
    nb = l // Q_BLOCK
    qb = jnp.moveaxis(q_lat.reshape(b, nb, Q_BLOCK, ATT_HEADS, 2, ATT_DH), 1, 0)
    o = lax.map(lambda qblk: diff_attend(qblk, k_all, v_all, lam), qb)
    return jnp.moveaxis(o, 0, 1).reshape(b, l, ATT_HEADS, ATT_VD)


def centred_dwconv(x, w, b):
    out = lax.conv_general_dilated(
        x, w[:, None, :].astype(x.dtype), window_strides=(1,),
        padding=((CONV_W // 2, CONV_W // 2),),
        dimension_numbers=("NWC", "WIO", "NWC"), feature_group_count=x.shape[-1])
    return out + b


def ssd_scan(xs, dt, a, bm, cm, h0):
    f32 = jnp.float32
    b, l = xs.shape[:2]
    nc = l // SSM_CHUNK
    x = xs.reshape(b, nc, SSM_CHUNK, SSM_GROUPS, SSM_HPG, SSM_HEADDIM).astype(f32)
    d = dt.reshape(b, nc, SSM_CHUNK, SSM_GROUPS, SSM_HPG).astype(f32)
    bc = bm.reshape(b, nc, SSM_CHUNK, SSM_GROUPS, SSM_STATE).astype(f32)
    cc = cm.reshape(b, nc, SSM_CHUNK, SSM_GROUPS, SSM_STATE).astype(f32)
    acum = jnp.cumsum(d * a, axis=2)
    xdt = x * d[..., None]
    tri = jnp.tril(jnp.ones((SSM_CHUNK, SSM_CHUNK), bool))[:, :, None, None]
    seg = acum[:, :, :, None] - acum[:, :, None, :]
    decay = jnp.where(tri, jnp.exp(jnp.where(tri, seg, 0.0)), 0.0)
    cb = jnp.einsum("bclgn,bcsgn->bclsg", cc, bc)
    y_diag = jnp.einsum("bclsg,bclsgr,bcsgrp->bclgrp", cb, decay, xdt)
    decay_end = jnp.exp(acum[:, :, -1:] - acum)
    states = jnp.einsum("bclgn,bclgr,bclgrp->bcgrpn", bc, decay_end, xdt)
    chunk_decay = jnp.exp(acum[:, :, -1])

    def step(h, inp):
        s, dec = inp
        return h * dec[..., None, None] + s, h

    h_fin, h_start = lax.scan(step, h0.astype(f32),
                              (jnp.moveaxis(states, 1, 0), jnp.moveaxis(chunk_decay, 1, 0)))
    h_start = jnp.moveaxis(h_start, 0, 1)
    y_off = jnp.einsum("bclgn,bcgrpn,bclgr->bclgrp", cc, h_start, jnp.exp(acum))
    y = (y_diag + y_off).reshape(b, l, SSM_GROUPS, SSM_HPG, SSM_HEADDIM)
    return y.astype(xs.dtype), h_fin


def ssm_prep(u, conv_w, conv_b, dt_bias):
    b, l = u.shape[:2]
    z, xbc, dt = jnp.split(u, [SSM_INNER, SSM_INNER + CONV_DIM], axis=-1)
    xbc = jax.nn.silu(centred_dwconv(xbc, conv_w, conv_b))
    xs, bm, cm = jnp.split(xbc, [SSM_INNER, SSM_INNER + SSM_GROUPS * SSM_STATE], axis=-1)
    xs = xs.reshape(b, l, SSM_GROUPS, SSM_HPG, SSM_HEADDIM)
    bm = bm.reshape(b, l, SSM_GROUPS, SSM_STATE)
    cm = cm.reshape(b, l, SSM_GROUPS, SSM_STATE)
    dt = jax.nn.softplus(dt.reshape(b, l, 2, SSM_GROUPS, SSM_HPG).astype(jnp.float32)
                         + dt_bias.reshape(2, SSM_GROUPS, SSM_HPG).astype(jnp.float32))
    return z, xs, bm, cm, dt


def bidir_ssd(xc, dtc, bc, cc, xl, dtl, bl, cl, a_log):
    a = -jnp.exp(a_log.astype(jnp.float32)).reshape(2, SSM_GROUPS, SSM_HPG)
    h0 = jnp.zeros((xc.shape[0], SSM_GROUPS, SSM_HPG, SSM_HEADDIM, SSM_STATE), jnp.float32)
    flip = lambda t: jnp.flip(t, axis=1)
    yc_f, hc_f = ssd_scan(xc, dtc[:, :, 0], a[0], bc, cc, h0)
    yl_f, _ = ssd_scan(xl, dtl[:, :, 0], a[0], bl, cl, hc_f)
    yc_b, hc_b = ssd_scan(flip(xc), flip(dtc[:, :, 1]), a[1], flip(bc), flip(cc), h0)
    yl_b, _ = ssd_scan(flip(xl), flip(dtl[:, :, 1]), a[1], flip(bl), flip(cl), hc_b)
    return yc_f + flip(yc_b), yl_f + flip(yl_b)


def ssm_finish(y, xs, z, d_skip, norm_w):
    b, l = z.shape[:2]
    y = y + d_skip.reshape(SSM_GROUPS, SSM_HPG)[..., None] * xs
    g = (y.reshape(b, l, SSM_INNER) * jax.nn.silu(z)).reshape(b, l, SSM_GROUPS, SSM_INNER // SSM_GROUPS)
    return rms_norm(g, norm_w.reshape(SSM_GROUPS, -1)).reshape(b, l, SSM_INNER)


def merge_branches(o_attn, o_ssm, gate_logits, w_ba, w_bs, w_out):
    g_attn, g_ssm = jnp.split(jax.nn.sigmoid(gate_logits), 2, axis=-1)
    return (g_attn * (o_attn @ w_ba) + g_ssm * (o_ssm @ w_bs)) @ w_out


def token_mixer(x_lat, x_ctx, mod_lat, mod_ctx, layer_idx, need_ctx, cos, sin,
                mix_norm, w_in, q_norm, k_norm, lq1, lk1, lq2, lk2, subln,
                conv_w, conv_b, dt_bias, a_log, d_skip, ssm_norm, w_ba, w_bs, w_out):
    b, l, _ = x_lat.shape
    lc = x_ctx.shape[1]
    p_lat = modulated_norm(x_lat, mix_norm, mod_lat, 1) @ w_in
    p_ctx = modulated_norm(x_ctx, mix_norm, mod_ctx, 1) @ w_in
    q_l, k_l, v_l, s_l, g_l = jnp.split(p_lat, IN_SPLITS, axis=-1)
    q_c, k_c, v_c, s_c, g_c = jnp.split(p_ctx, IN_SPLITS, axis=-1)

    lam_init = 0.8 - 0.6 * math.exp(-0.3 * layer_idx)
    f32 = jnp.float32
    lam = (jnp.exp(jnp.sum(lq1.astype(f32) * lk1.astype(f32)))
           - jnp.exp(jnp.sum(lq2.astype(f32) * lk2.astype(f32))) + lam_init)
    qk_shape = lambda t, n: t.reshape(b, n, ATT_HEADS, 2, ATT_DH)
    q_l = apply_rope(rms_norm(qk_shape(q_l, l), q_norm), cos, sin)
    k_l = apply_rope(rms_norm(qk_shape(k_l, l), k_norm), cos, sin)
    q_c = rms_norm(qk_shape(q_c, lc), q_norm)
    k_c = rms_norm(qk_shape(k_c, lc), k_norm)
    v_l = v_l.reshape(b, l, ATT_HEADS, ATT_VD)
    v_c = v_c.reshape(b, lc, ATT_HEADS, ATT_VD)
    k_all = jnp.concatenate([k_l, k_c], axis=1)
    v_all = jnp.concatenate([v_l, v_c], axis=1)
    att_post = lambda o: (rms_norm(o, subln) * (1.0 - lam_init)).reshape(o.shape[0], o.shape[1], ATT_WIDTH)
    o_attn_l = att_post(latent_diff_attention(q_l, k_all, v_all, lam))

    z_c, xs_c, bm_c, cm_c, dt_c = ssm_prep(s_c, conv_w, conv_b, dt_bias)
    z_l, xs_l, bm_l, cm_l, dt_l = ssm_prep(s_l, conv_w, conv_b, dt_bias)
    y_c, y_l = bidir_ssd(xs_c, dt_c, bm_c, cm_c, xs_l, dt_l, bm_l, cm_l, a_log)
    o_ssm_l = ssm_finish(y_l, xs_l, z_l, d_skip, ssm_norm)

    x_lat = x_lat + mod_lat[:, :, 5] * merge_branches(o_attn_l, o_ssm_l, g_l, w_ba, w_bs, w_out)
    if not need_ctx:
        return x_lat, None
    o_attn_c = att_post(diff_attend(q_c, k_c, v_c, lam))
    o_ssm_c = ssm_finish(y_c, xs_c, z_c, d_skip, ssm_norm)
    x_ctx = x_ctx + mod_ctx[:, :, 5] * merge_branches(o_attn_c, o_ssm_c, g_c, w_ba, w_bs, w_out)
    return x_lat, x_ctx


def setup_inputs(seed: int = 0) -> dict:
    key = jax.random.key(seed)
    ks = jax.random.split(key, 32)
    f32 = jnp.float32

    def nrm(k, shape, std=1.0):
        return std * jax.random.normal(k, shape, f32)

    def dense(k, shape, fan_in):
        return nrm(k, shape, fan_in ** -0.5)

    def gain(k, shape):
        return 1.0 + nrm(k, shape, 0.05)

    dt0 = jnp.exp(jax.random.uniform(ks[20], (DEPTH, 2, SSM_HEADS), f32, math.log(1e-3), math.log(1e-1)))
    return {
        "x": nrm(ks[0], (BATCH, SEQ, D_MODEL)),
        "c": nrm(ks[1], (BATCH, D_MODEL)),
        "ctx": nrm(ks[2], (BATCH, CTX_LEN, D_MODEL)),
        "c_ctx": nrm(ks[3], (D_MODEL,)),
        "ada_w": dense(ks[4], (DEPTH, D_MODEL, N_MOD * D_MODEL), D_MODEL),
        "ada_b": nrm(ks[5], (DEPTH, N_MOD * D_MODEL), 0.02),
        "ffn1_norm": gain(ks[6], (DEPTH, D_MODEL)),
        "ffn1_w_gu": dense(ks[7], (DEPTH, D_MODEL, 2 * D_FF), D_MODEL),
        "ffn1_w_down": dense(ks[8], (DEPTH, D_FF, D_MODEL), D_FF),
        "mix_norm": gain(ks[9], (DEPTH, D_MODEL)),
        "w_in": dense(ks[10], (DEPTH, D_MODEL, IN_COLS), D_MODEL),
        "q_norm": gain(ks[11], (DEPTH, ATT_DH)),
        "k_norm": gain(ks[12], (DEPTH, ATT_DH)),
        "lambda_q1": nrm(ks[13], (DEPTH, ATT_DH), 0.1),
        "lambda_k1": nrm(ks[14], (DEPTH, ATT_DH), 0.1),
        "lambda_q2": nrm(ks[15], (DEPTH, ATT_DH), 0.1),
        "lambda_k2": nrm(ks[16], (DEPTH, ATT_DH), 0.1),
        "attn_subln": gain(ks[17], (DEPTH, ATT_VD)),
        "conv_w": dense(ks[18], (DEPTH, CONV_W, CONV_DIM), CONV_W),
        "conv_b": nrm(ks[19], (DEPTH, CONV_DIM), 0.02),
        "dt_bias": dt0 + jnp.log(-jnp.expm1(-dt0)),
        "a_log": jnp.log(jax.random.uniform(ks[21], (DEPTH, 2, SSM_HEADS), f32, 1.0, 16.0)),
        "d_skip": 1.0 + nrm(ks[22], (DEPTH, SSM_HEADS), 0.1),
        "ssm_norm": gain(ks[23], (DEPTH, SSM_INNER)),
        "w_branch_attn": dense(ks[24], (DEPTH, ATT_WIDTH, D_MODEL), ATT_WIDTH),
        "w_branch_ssm": dense(ks[25], (DEPTH, SSM_INNER, D_MODEL), SSM_INNER),
        "w_out": dense(ks[26], (DEPTH, D_MODEL, D_MODEL), D_MODEL),
        "ffn2_norm": gain(ks[27], (DEPTH, D_MODEL)),
        "ffn2_w_gu": dense(ks[28], (DEPTH, D_MODEL, 2 * D_FF), D_MODEL),
        "ffn2_w_down": dense(ks[29], (DEPTH, D_FF, D_MODEL), D_FF),
    }


def reference(x, c, ctx, c_ctx, ada_w, ada_b, ffn1_norm, ffn1_w_gu, ffn1_w_down,
              mix_norm, w_in, q_norm, k_norm, lambda_q1, lambda_k1, lambda_q2, lambda_k2,
              attn_subln, conv_w, conv_b, dt_bias, a_log, d_skip, ssm_norm,
              w_branch_attn, w_branch_ssm, w_out, ffn2_norm, ffn2_w_gu, ffn2_w_down):
    l = x.shape[1]
    rows = l // GRID_W
    cos, sin = axial_rope(rows)
    for i in range(DEPTH):
        last = i == DEPTH - 1
        mod_l = adaln_params(c, ada_w[i], ada_b[i])
        mod_c = adaln_params(c_ctx[None], ada_w[i], ada_b[i])
        x = ffn_sublayer(x, mod_l, 0, ffn1_norm[i], ffn1_w_gu[i], ffn1_w_down[i])
        ctx = ffn_sublayer(ctx, mod_c, 0, ffn1_norm[i], ffn1_w_gu[i], ffn1_w_down[i])
        x, ctx_new = token_mixer(
            x, ctx, mod_l, mod_c, i, not last, cos, sin,
            mix_norm[i], w_in[i], q_norm[i], k_norm[i],
            lambda_q1[i], lambda_k1[i], lambda_q2[i], lambda_k2[i], attn_subln[i],
            conv_w[i], conv_b[i], dt_bias[i], a_log[i], d_skip[i], ssm_norm[i],
            w_branch_attn[i], w_branch_ssm[i], w_out[i])
        x = ffn_sublayer(x, mod_l, 2, ffn2_norm[i], ffn2_w_gu[i], ffn2_w_down[i])
        if not last:
            ctx = ffn_sublayer(ctx_new, mod_c, 2, ffn2_norm[i], ffn2_w_gu[i], ffn2_w_down[i])
    return x
```

```python
import functools
import math

import jax
import jax.numpy as jnp
from jax import lax
from jax.experimental import pallas as pl
from jax.experimental.pallas import tpu as pltpu

f32 = jnp.float32
bf16 = jnp.bfloat16

EPS = 1e-6
N_MOD = 9
FFN_RESIDUAL = 0.5
GRID_W = 64
ROPE_BASE = 10000.0

ATT_HEADS = 8
ATT_DH = 64
ATT_VD = 2 * ATT_DH
ATT_WIDTH = ATT_HEADS * ATT_VD

SSM_INNER = 2048
SSM_HEADDIM = 64
SSM_HEADS = SSM_INNER // SSM_HEADDIM
SSM_GROUPS = 4
SSM_HPG = SSM_HEADS // SSM_GROUPS
SSM_STATE = 128
SSM_CHUNK = 128
CONV_W = 5
CONV_DIM = SSM_INNER + 2 * SSM_GROUPS * SSM_STATE
GROUP_W = SSM_HPG * SSM_HEADDIM

LANE = 128
COND_ROWS = 16
HALO = 16
NEG_BIG = -1e30

P_Q, P_K, P_V = 0, ATT_WIDTH, 2 * ATT_WIDTH
P_XBC = 3 * ATT_WIDTH
P_Z = P_XBC + CONV_DIM
P_GA = P_Z + SSM_INNER
P_COLS = P_GA

VMEM_LIMIT = 56 * 1024 * 1024


def _cparams(sem):
    return pltpu.CompilerParams(dimension_semantics=sem, vmem_limit_bytes=VMEM_LIMIT)


def _silu(v):
    return v * jax.nn.sigmoid(v)


def _mod_norm(x, nw, shift, scale):
    y = x * lax.rsqrt(jnp.mean(x * x, axis=-1, keepdims=True) + EPS)
    return (y * nw) * (1.0 + scale) + shift


def _adaln_kernel(c_ref, w_ref, b_ref, o_ref):
    c = c_ref[...]
    s = _silu(c).astype(bf16)
    o_ref[...] = jnp.dot(s, w_ref[...].astype(bf16), preferred_element_type=f32) + b_ref[...]


def _adaln(conds, ada_w, ada_b):
    depth, d, n = ada_w.shape
    tn = 1024
    return pl.pallas_call(
        _adaln_kernel,
        grid=(depth, n // tn),
        in_specs=[
            pl.BlockSpec((COND_ROWS, d), lambda l, j: (0, 0)),
            pl.BlockSpec((None, d, tn), lambda l, j: (l, 0, j)),
            pl.BlockSpec((None, 1, tn), lambda l, j: (l, 0, j)),
        ],
        out_specs=pl.BlockSpec((None, COND_ROWS, tn), lambda l, j: (l, 0, j)),
        out_shape=jax.ShapeDtypeStruct((depth, COND_ROWS, n), f32),
        compiler_params=_cparams(("parallel", "parallel")),
    )(conds, ada_w, ada_b.reshape(depth, 1, n))


def _ffn_kernel(x_ref, sh_ref, sc_ref, gt_ref, nw_ref, wg_ref, wv_ref, wd_ref, o_ref, xn_ref, acc_ref):
    j = pl.program_id(1)

    @pl.when(j == 0)
    def _():
        xn_ref[...] = _mod_norm(x_ref[...], nw_ref[...], sh_ref[...], sc_ref[...]).astype(bf16)
        acc_ref[...] = jnp.zeros_like(acc_ref)

    xn = xn_ref[...]
    g = jnp.dot(xn, wg_ref[...], preferred_element_type=f32)
    v = jnp.dot(xn, wv_ref[...], preferred_element_type=f32)
    a = (_silu(g) * v).astype(bf16)
    acc_ref[...] += jnp.dot(a, wd_ref[...], preferred_element_type=f32)

    @pl.when(j == pl.num_programs(1) - 1)
    def _():
        o_ref[...] = x_ref[...] + (FFN_RESIDUAL * gt_ref[...]) * acc_ref[...]


def _mod_spec(d, row_fn, col):
    return pl.BlockSpec((None, 1, d), lambda i, j: (row_fn(i), 0, col))


def _ffn(xall, n_rows, mod3, row_fn, slot, norm_w, w_gu, w_down, tm, tf):
    d = xall.shape[1]
    ff = w_down.shape[0]
    nf = ff // tf
    return pl.pallas_call(
        _ffn_kernel,
        grid=(n_rows // tm, nf),
        in_specs=[
            pl.BlockSpec((tm, d), lambda i, j: (i, 0)),
            _mod_spec(d, row_fn, 3 * slot),
            _mod_spec(d, row_fn, 3 * slot + 1),
            _mod_spec(d, row_fn, 3 * slot + 2),
            pl.BlockSpec((1, d), lambda i, j: (0, 0)),
            pl.BlockSpec((d, tf), lambda i, j: (0, j)),
            pl.BlockSpec((d, tf), lambda i, j: (0, nf + j)),
            pl.BlockSpec((tf, d), lambda i, j: (j, 0)),
        ],
        out_specs=pl.BlockSpec((tm, d), lambda i, j: (i, 0)),
        out_shape=jax.ShapeDtypeStruct((n_rows, d), f32),
        scratch_shapes=[pltpu.VMEM((tm, d), bf16), pltpu.VMEM((tm, d), f32)],
        compiler_params=_cparams(("parallel", "arbitrary")),
    )(xall, mod3, mod3, mod3, norm_w.reshape(1, d), w_gu, w_gu, w_down)


QK_TILES = 2 * ATT_WIDTH // 512


def _inproj_kernel(x_ref, sh_ref, sc_ref, nw_ref, w_ref, wdt_ref, cos_ref, sin_ref, g_ref, qkw_ref,
                   p_ref, dt_ref, xn_ref, *, n_plain_end):
    j = pl.program_id(1)

    @pl.when(j == 0)
    def _():
        xn = _mod_norm(x_ref[...], nw_ref[...], sh_ref[...], sc_ref[...]).astype(bf16)
        xn_ref[...] = xn
        dt_ref[...] = jnp.dot(xn, wdt_ref[...], preferred_element_type=f32)

    y = jnp.dot(xn_ref[...], w_ref[...], preferred_element_type=f32)

    @pl.when(j < QK_TILES)
    def _():
        ss = jnp.dot((y * y).astype(bf16), g_ref[...], preferred_element_type=f32)
        yn = y * lax.rsqrt(ss * (1.0 / ATT_DH) + EPS) * qkw_ref[...]
        partner = jnp.concatenate(
            [pltpu.roll(yn[:, h * LANE:(h + 1) * LANE], LANE // 2, 1) for h in range(yn.shape[1] // LANE)],
            axis=1)
        p_ref[...] = (yn * cos_ref[...] + partner * sin_ref[...]).astype(bf16)

    @pl.when(jnp.logical_and(j >= QK_TILES, j < n_plain_end))
    def _():
        p_ref[...] = y.astype(bf16)

    @pl.when(j >= n_plain_end)
    def _():
        p_ref[...] = jax.nn.sigmoid(y).astype(bf16)


def _inproj(xall, mod3, row_fn, rope_fn, mix_norm, w_p, w_dt, cos_tab, sin_tab, gmat, qkw, tm):
    t, d = xall.shape
    n = w_p.shape[1]
    tn = 512
    kern = functools.partial(_inproj_kernel, n_plain_end=P_GA // tn)
    return pl.pallas_call(
        kern,
        grid=(t // tm, n // tn),
        in_specs=[
            pl.BlockSpec((tm, d), lambda i, j: (i, 0)),
            _mod_spec(d, row_fn, 3),
            _mod_spec(d, row_fn, 4),
            pl.BlockSpec((1, d), lambda i, j: (0, 0)),
            pl.BlockSpec((d, tn), lambda i, j: (0, j)),
            pl.BlockSpec((d, 2 * LANE), lambda i, j: (0, 0)),
            pl.BlockSpec((tm, tn), lambda i, j: (rope_fn(i), 0)),
            pl.BlockSpec((tm, tn), lambda i, j: (rope_fn(i), 0)),
            pl.BlockSpec((tn, tn), lambda i, j: (0, 0)),
            pl.BlockSpec((None, 1, tn), lambda i, j: (jnp.minimum(j // (QK_TILES // 2), 1), 0, 0)),
        ],
        out_specs=[
            pl.BlockSpec((tm, tn), lambda i, j: (i, j)),
            pl.BlockSpec((tm, 2 * LANE), lambda i, j: (i, 0)),
        ],
        out_shape=[jax.ShapeDtypeStruct((t, n), bf16), jax.ShapeDtypeStruct((t, 2 * LANE), f32)],
        scratch_shapes=[pltpu.VMEM((tm, d), bf16)],
        compiler_params=_cparams(("parallel", "arbitrary")),
    )(xall, mod3, mod3, mix_norm.reshape(1, d), w_p, w_dt, cos_tab, sin_tab, gmat, qkw)


def _attn_kernel(*refs, n_seg, lam_init):
    q_ref = refs[0]
    kv_refs = refs[1:1 + 2 * n_seg]
    lq1_ref, lk1_ref, lq2_ref, lk2_ref, sub_ref, o_ref = refs[1 + 2 * n_seg:]

    lam = (jnp.exp(jnp.sum(lq1_ref[...] * lk1_ref[...], axis=-1, keepdims=True))
           - jnp.exp(jnp.sum(lq2_ref[...] * lk2_ref[...], axis=-1, keepdims=True)) + lam_init)

    q = q_ref[...]
    comp = (lax.broadcasted_iota(jnp.int32, q.shape, 1) // (ATT_DH // 2)) % 2
    qc = [jnp.where(comp == 0, q, jnp.zeros_like(q)), jnp.where(comp == 1, q, jnp.zeros_like(q))]

    s = [[lax.dot_general(qc[c], kv_refs[2 * g][...], (((1,), (1,)), ((), ())), preferred_element_type=f32)
          for g in range(n_seg)] for c in range(2)]
    coef = []
    e = []
    for c in range(2):
        m = functools.reduce(jnp.maximum, [jnp.max(sg, axis=-1, keepdims=True) for sg in s[c]])
        ec = [jnp.exp(sg - m) for sg in s[c]]
        den = functools.reduce(lambda u, v: u + v, [jnp.sum(eg, axis=-1, keepdims=True) for eg in ec])
        e.append(ec)
        coef.append(1.0 / den)
    coef[1] = coef[1] * lam
    o = None
    for g in range(n_seg):
        a = (e[0][g] * coef[0] - e[1][g] * coef[1]).astype(bf16)
        og = jnp.dot(a, kv_refs[2 * g + 1][...], preferred_element_type=f32)
        o = og if o is None else o + og
    on = o * lax.rsqrt(jnp.mean(o * o, axis=-1, keepdims=True) + EPS) * sub_ref[...] * (1.0 - lam_init)
    o_ref[...] = on.astype(bf16)


def _attention(p, n_q_rows, q_row0, lq, tq, segs, lam_init, lam_params, subln):
    nb = n_q_rows // lq
    nqt = lq // tq
    in_specs = [pl.BlockSpec((tq, LANE), lambda b, h, i: (q_row0 // tq + b * nqt + i, P_Q // LANE + h))]
    args = [p]
    for row0, ls in segs:
        in_specs.append(pl.BlockSpec((ls, LANE), functools.partial(
            lambda b, h, i, r0, col: (r0 + b, col + h), r0=row0 // ls, col=P_K // LANE)))
        in_specs.append(pl.BlockSpec((ls, LANE), functools.partial(
            lambda b, h, i, r0, col: (r0 + b, col + h), r0=row0 // ls, col=P_V // LANE)))
        args += [p, p]
    for lp in lam_params:
        in_specs.append(pl.BlockSpec((1, ATT_DH), lambda b, h, i: (0, 0)))
        args.append(lp.reshape(1, ATT_DH))
    in_specs.append(pl.BlockSpec((1, ATT_VD), lambda b, h, i: (0, 0)))
    args.append(subln.reshape(1, ATT_VD))
    kern = functools.partial(_attn_kernel, n_seg=len(segs), lam_init=lam_init)
    return pl.pallas_call(
        kern,
        grid=(nb, ATT_HEADS, nqt),
        in_specs=in_specs,
        out_specs=pl.BlockSpec((tq, LANE), lambda b, h, i: (b * nqt + i, h)),
        out_shape=jax.ShapeDtypeStruct((n_q_rows, ATT_WIDTH), bf16),
        compiler_params=_cparams(("parallel", "parallel", "arbitrary")),
    )(*args)


def _conv_kernel(x_ref, prev_ref, next_ref, w_ref, b_ref, o_ref, pad_ref, *, rows, n_lat_blocks, lat_per_seq,
                 ctx_per_seq):
    i = pl.program_id(0)
    is_lat = i < n_lat_blocks
    per = jnp.where(is_lat, lat_per_seq, ctx_per_seq)
    pos = jnp.where(is_lat, i % lat_per_seq, (i - n_lat_blocks) % ctx_per_seq)
    keep_prev = (pos != 0).astype(f32)
    keep_next = (pos != per - 1).astype(f32)
    pad_ref[0:HALO, :] = prev_ref[...].astype(f32) * keep_prev
    pad_ref[HALO:HALO + rows, :] = x_ref[...].astype(f32)
    pad_ref[HALO + rows:2 * HALO + rows, :] = next_ref[...].astype(f32) * keep_next
    acc = jnp.zeros((rows, x_ref.shape[1]), f32) + b_ref[...]
    for k in range(CONV_W):
        acc = acc + w_ref[k:k + 1, :] * pad_ref[pl.ds(HALO - CONV_W // 2 + k, rows), :]
    o_ref[...] = _silu(acc).astype(bf16)


def _conv(p, conv_w, conv_b, t_lat, l, lc):
    t = p.shape[0]
    rows = math.gcd(l, lc)
    rows = min(rows, 256)
    tc = 1024
    hb = rows // HALO
    c0 = P_XBC // tc
    kern = functools.partial(_conv_kernel, rows=rows, n_lat_blocks=t_lat // rows, lat_per_seq=l // rows,
                             ctx_per_seq=lc // rows)
    return pl.pallas_call(
        kern,
        grid=(t // rows, CONV_DIM // tc),
        in_specs=[
            pl.BlockSpec((rows, tc), lambda i, j: (i, c0 + j)),
            pl.BlockSpec((HALO, tc), lambda i, j: (jnp.maximum(i * hb - 1, 0), c0 + j)),
            pl.BlockSpec((HALO, tc), lambda i, j: (jnp.minimum((i + 1) * hb, t // HALO - 1), c0 + j)),
            pl.BlockSpec((CONV_W, tc), lambda i, j: (0, j)),
            pl.BlockSpec((1, tc), lambda i, j: (0, j)),
        ],
        out_specs=pl.BlockSpec((rows, tc), lambda i, j: (i, j)),
        out_shape=jax.ShapeDtypeStruct((t, CONV_DIM), bf16),
        scratch_shapes=[pltpu.VMEM((rows + 2 * HALO, tc), f32)],
        compiler_params=_cparams(("parallel", "parallel")),
    )(p, p, p, conv_w, conv_b.reshape(1, CONV_DIM))


def _ssd_kernel(xs_ref, b_ref, c_ref, dt_ref, bias_ref, alog_ref, e_ref, y_ref, h_ref):
    d = pl.program_id(0)
    t = pl.program_id(2)
    ck = SSM_CHUNK
    hi = lax.Precision.HIGHEST

    @pl.when(t == 0)
    def _():
        h_ref[...] = jnp.zeros_like(h_ref)

    dtv = jax.nn.softplus(dt_ref[...] + bias_ref[...])
    adt = dtv * (-jnp.exp(alog_ref[...]))
    li = lax.broadcasted_iota(jnp.int32, (ck, ck), 0)
    si = lax.broadcasted_iota(jnp.int32, (ck, ck), 1)
    sgn = jnp.where(d == 0, 1, -1)
    mask = (si - li) * sgn <= 0
    mask_t = (li - si) * sgn <= 0
    acum = jnp.dot(mask.astype(f32), adt, precision=hi, preferred_element_type=f32)
    adt_t = adt.T
    dtv_t = dtv.T
    acum_t = jnp.dot(adt_t, mask_t.astype(f32), precision=hi, preferred_element_type=f32)
    tot_t = jnp.sum(adt_t, axis=1, keepdims=True)
    w_t = jnp.exp(tot_t - acum_t) * dtv_t
    e_col = jnp.exp(acum)
    tot_row = jnp.sum(adt, axis=0, keepdims=True)
    dec = jnp.dot(jnp.broadcast_to(jnp.exp(tot_row), (8, LANE)), e_ref[...], precision=hi,
                  preferred_element_type=f32)[0:1, :]

    lane_head = lax.broadcasted_iota(jnp.int32, (ck, GROUP_W), 1) // SSM_HEADDIM
    for g in range(SSM_GROUPS):
        bg = b_ref[:, g * SSM_STATE:(g + 1) * SSM_STATE]
        cg = c_ref[:, g * SSM_STATE:(g + 1) * SSM_STATE]
        cb = lax.dot_general(cg, bg, (((1,), (1,)), ((), ())), preferred_element_type=f32)
        bg_t = bg.astype(f32).T
        cg_f = cg.astype(f32)
        xg = xs_ref[:, g * GROUP_W:(g + 1) * GROUP_W]
        hg = h_ref[g]
        xm = jnp.concatenate([jnp.where(lane_head == r, xg, jnp.zeros_like(xg)) for r in range(SSM_HPG)], axis=0)
        hm = jnp.concatenate([jnp.where(lane_head == r, hg, 0.0).astype(bf16) for r in range(SSM_HPG)], axis=0)
        lhs_diag, lhs_off, lhs_st = [], [], []
        for r in range(SSM_HPG):
            hh = g * SSM_HPG + r
            seg = jnp.broadcast_to(acum[:, hh:hh + 1], (ck, ck)) - acum_t[hh:hh + 1, :]
            decay = jnp.exp(jnp.where(mask, seg, NEG_BIG))
            lhs_diag.append((cb * decay * dtv_t[hh:hh + 1, :]).astype(bf16))
            lhs_off.append((cg_f * jnp.broadcast_to(e_col[:, hh:hh + 1], (ck, SSM_STATE))).astype(bf16))
            lhs_st.append((bg_t * w_t[hh:hh + 1, :]).astype(bf16))
        y = jnp.dot(jnp.concatenate(lhs_diag + lhs_off, axis=1), jnp.concatenate([xm, hm], axis=0),
                    preferred_element_type=f32)
        y_ref[:, g * GROUP_W:(g + 1) * GROUP_W] = y
        st = jnp.dot(jnp.concatenate(lhs_st, axis=1), xm, preferred_element_type=f32)
        h_ref[g] = hg * dec[:, g * GROUP_W:(g + 1) * GROUP_W] + st


def _ssd(xbc, dt_raw, dt_bias, a_log, expand, nb, t_lat, l, lc):
    t = xbc.shape[0]
    ck = SSM_CHUNK
    nck, nlk = lc // ck, l // ck

    def rowblk(d, b, s):
        tc = jnp.where(d == 0, s, nck - 1 - s)
        tl = jnp.where(d == 0, s - nck, nlk - 1 - (s - nck))
        return jnp.where(s < nck, t_lat // ck + b * nck + tc, b * nlk + tl)

    bc0 = SSM_INNER // (SSM_GROUPS * SSM_STATE)
    pad = lambda v: jnp.pad(v.astype(f32), ((0, 0), (0, LANE - SSM_HEADS))).reshape(2, 1, LANE)
    return pl.pallas_call(
        _ssd_kernel,
        grid=(2, nb, nck + nlk),
        in_specs=[
            pl.BlockSpec((ck, SSM_INNER), lambda d, b, s: (rowblk(d, b, s), 0)),
            pl.BlockSpec((ck, SSM_GROUPS * SSM_STATE), lambda d, b, s: (rowblk(d, b, s), bc0)),
            pl.BlockSpec((ck, SSM_GROUPS * SSM_STATE), lambda d, b, s: (rowblk(d, b, s), bc0 + 1)),
            pl.BlockSpec((ck, LANE), lambda d, b, s: (rowblk(d, b, s), d)),
            pl.BlockSpec((None, 1, LANE), lambda d, b, s: (d, 0, 0)),
            pl.BlockSpec((None, 1, LANE), lambda d, b, s: (d, 0, 0)),
            pl.BlockSpec((LANE, SSM_INNER), lambda d, b, s: (0, 0)),
        ],
        out_specs=pl.BlockSpec((None, ck, SSM_INNER), lambda d, b, s: (d, rowblk(d, b, s), 0)),
        out_shape=jax.ShapeDtypeStruct((2, t, SSM_INNER), f32),
        scratch_shapes=[pltpu.VMEM((SSM_GROUPS, SSM_STATE, GROUP_W), f32)],
        compiler_params=_cparams(("parallel", "parallel", "arbitrary")),
    )(xbc, xbc, xbc, dt_raw, pad(dt_bias), pad(a_log), expand)


def _finish_kernel(yf_ref, yb_ref, xs_ref, z_ref, dsk_ref, nw_ref, o_ref):
    y = yf_ref[...] + yb_ref[...] + dsk_ref[...] * xs_ref[...].astype(f32)
    gz = y * _silu(z_ref[...].astype(f32))
    gw = SSM_INNER // SSM_GROUPS
    for g in range(SSM_GROUPS):
        v = gz[:, g * gw:(g + 1) * gw]
        vn = v * lax.rsqrt(jnp.mean(v * v, axis=-1, keepdims=True) + EPS) * nw_ref[:, g * gw:(g + 1) * gw]
        o_ref[:, g * gw:(g + 1) * gw] = vn.astype(bf16)


def _ssm_finish(y2, xbc, p, n_rows, d_skip_lanes, ssm_norm, tr):
    si = SSM_INNER
    return pl.pallas_call(
        _finish_kernel,
        grid=(n_rows // tr,),
        in_specs=[
            pl.BlockSpec((None, tr, si), lambda i: (0, i, 0)),
            pl.BlockSpec((None, tr, si), lambda i: (1, i, 0)),
            pl.BlockSpec((tr, si), lambda i: (i, 0)),
            pl.BlockSpec((tr, si), lambda i: (i, P_Z // si)),
            pl.BlockSpec((1, si), lambda i: (0, 0)),
            pl.BlockSpec((1, si), lambda i: (0, 0)),
        ],
        out_specs=pl.BlockSpec((tr, si), lambda i: (i, 0)),
        out_shape=jax.ShapeDtypeStruct((n_rows, si), bf16),
        compiler_params=_cparams(("parallel",)),
    )(y2, y2, xbc, p, d_skip_lanes, ssm_norm.reshape(1, si))


def _merge_kernel(oa_ref, os_ref, ga_ref, gs_ref, wa_ref, ws_ref, u_ref):
    ua = jnp.dot(oa_ref[...], wa_ref[...], preferred_element_type=f32)
    us = jnp.dot(os_ref[...], ws_ref[...], preferred_element_type=f32)
    u_ref[...] = (ga_ref[...].astype(f32) * ua + gs_ref[...].astype(f32) * us).astype(bf16)


def _merge(o_attn, o_ssm, p, n_rows, w_ba, w_bs, tm):
    d = w_ba.shape[1]
    tn = 512
    return pl.pallas_call(
        _merge_kernel,
        grid=(n_rows // tm, d // tn),
        in_specs=[
            pl.BlockSpec((tm, ATT_WIDTH), lambda i, j: (i, 0)),
            pl.BlockSpec((tm, SSM_INNER), lambda i, j: (i, 0)),
            pl.BlockSpec((tm, tn), lambda i, j: (i, P_GA // tn + j)),
            pl.BlockSpec((tm, tn), lambda i, j: (i, (P_GA + d) // tn + j)),
            pl.BlockSpec((ATT_WIDTH, tn), lambda i, j: (0, j)),
            pl.BlockSpec((SSM_INNER, tn), lambda i, j: (0, j)),
        ],
        out_specs=pl.BlockSpec((tm, tn), lambda i, j: (i, j)),
        out_shape=jax.ShapeDtypeStruct((n_rows, d), bf16),
        compiler_params=_cparams(("parallel", "arbitrary")),
    )(o_attn, o_ssm, p, p, w_ba, w_bs)


def _outproj_kernel(x_ref, u_ref, gt_ref, w_ref, o_ref):
    o_ref[...] = x_ref[...] + gt_ref[...] * jnp.dot(u_ref[...], w_ref[...], preferred_element_type=f32)


def _outproj(xall, u, n_rows, mod3, row_fn, w_out, tm):
    d = w_out.shape[1]
    tn = 512
    return pl.pallas_call(
        _outproj_kernel,
        grid=(n_rows // tm, d // tn),
        in_specs=[
            pl.BlockSpec((tm, tn), lambda i, j: (i, j)),
            pl.BlockSpec((tm, d), lambda i, j: (i, 0)),
            pl.BlockSpec((None, 1, tn), lambda i, j: (row_fn(i), 0, 5 * (d // tn) + j)),
            pl.BlockSpec((d, tn), lambda i, j: (0, j)),
        ],
        out_specs=pl.BlockSpec((tm, tn), lambda i, j: (i, j)),
        out_shape=jax.ShapeDtypeStruct((n_rows, d), f32),
        compiler_params=_cparams(("parallel", "arbitrary")),
    )(xall, u, mod3, w_out)


def _qk_perm():
    lane = jnp.arange(LANE)
    par, comp, freq = lane // 64, (lane // 32) % 2, lane % 32
    within = comp * ATT_DH + 2 * freq + par
    return (jnp.arange(ATT_HEADS)[:, None] * LANE + within[None, :]).reshape(-1), (2 * freq + par)


def _rope_tables(l, tm, tn):
    n_freq = ATT_DH // 4
    rows = l // GRID_W
    inv = ROPE_BASE ** (-jnp.arange(n_freq, dtype=f32) / n_freq)
    row = jnp.repeat(jnp.arange(rows, dtype=f32), GRID_W)
    col = jnp.tile(jnp.arange(GRID_W, dtype=f32), rows)
    ang = jnp.concatenate([row[:, None] * inv, col[:, None] * inv], axis=-1)
    cos, sin = jnp.cos(ang), jnp.sin(ang)
    cos_l = jnp.tile(cos, (1, tn // (ATT_DH // 2)))
    sin_l = jnp.tile(jnp.concatenate([-sin, -sin, sin, sin], axis=-1), (1, tn // LANE))
    cos_tab = jnp.concatenate([cos_l, jnp.ones((tm, tn), f32)], axis=0)
    sin_tab = jnp.concatenate([sin_l, jnp.zeros((tm, tn), f32)], axis=0)
    return cos_tab, sin_tab


def kernel(x, c, ctx, c_ctx, ada_w, ada_b, ffn1_norm, ffn1_w_gu, ffn1_w_down, mix_norm, w_in, q_norm, k_norm,
           lambda_q1, lambda_k1, lambda_q2, lambda_k2, attn_subln, conv_w, conv_b, dt_bias, a_log, d_skip,
           ssm_norm, w_branch_attn, w_branch_ssm, w_out, ffn2_norm, ffn2_w_gu, ffn2_w_down):
    nb, l, d = x.shape
    lc = ctx.shape[1]
    depth = ada_w.shape[0]
    t_lat, t_ctx = nb * l, nb * lc
    t = t_lat + t_ctx
    tm = min(512, l)
    tf = 512
    tq = min(256, l)
    assert l % tm == 0 and t_ctx % tm == 0 and nb < COND_ROWS and l % GRID_W == 0
    n_lat_tiles = t_lat // tm
    tiles_per_seq = l // tm

    xall = jnp.concatenate([x.reshape(t_lat, d), ctx.reshape(t_ctx, d)], axis=0)
    conds = jnp.zeros((COND_ROWS, d), f32).at[:nb].set(c).at[nb].set(c_ctx)
    mod = _adaln(conds, ada_w, ada_b)
    mod3 = mod.reshape(depth * COND_ROWS, 1, N_MOD * d)

    perm, norm_idx = _qk_perm()
    cos_tab, sin_tab = _rope_tables(l, tm, 512)
    rope_fn = lambda i: jnp.where(i < n_lat_tiles, i % tiles_per_seq, tiles_per_seq)
    lane512 = jnp.arange(512)
    grp = (lane512 // LANE) * 2 + (lane512 // 32) % 2
    gmat = (grp[:, None] == grp[None, :]).astype(bf16)
    expand = (jnp.arange(LANE)[:, None] == (jnp.arange(SSM_INNER) // SSM_HEADDIM)[None, :]).astype(f32)

    s0 = 3 * ATT_WIDTH
    dt0 = s0 + SSM_INNER + CONV_DIM
    g0 = dt0 + 2 * SSM_HEADS

    for i in range(depth):
        last = i == depth - 1
        row_fn = functools.partial(
            lambda ti, base: base + jnp.where(ti < n_lat_tiles, ti // tiles_per_seq, nb), base=i * COND_ROWS)
        lam_init = 0.8 - 0.6 * math.exp(-0.3 * i)

        w = w_in[i]
        w_p = jnp.concatenate([
            w[:, perm], w[:, ATT_WIDTH + perm], w[:, 2 * ATT_WIDTH:s0],
            w[:, s0 + SSM_INNER:dt0], w[:, s0:s0 + SSM_INNER], w[:, g0:]], axis=1).astype(bf16)
        zpad = jnp.zeros((d, LANE - SSM_HEADS), f32)
        w_dt = jnp.concatenate([w[:, dt0:dt0 + SSM_HEADS], zpad, w[:, dt0 + SSM_HEADS:g0], zpad],
                               axis=1).astype(bf16)
        qkw = jnp.stack([jnp.tile(q_norm[i][norm_idx] * (ATT_DH ** -0.5), 4),
                         jnp.tile(k_norm[i][norm_idx], 4)]).reshape(2, 1, 512)
        w_gu1, w_d1 = ffn1_w_gu[i].astype(bf16), ffn1_w_down[i].astype(bf16)
        w_gu2, w_d2 = ffn2_w_gu[i].astype(bf16), ffn2_w_down[i].astype(bf16)
        w_ba, w_bs, w_o = w_branch_attn[i].astype(bf16), w_branch_ssm[i].astype(bf16), w_out[i].astype(bf16)
        lam_params = (lambda_q1[i], lambda_k1[i], lambda_q2[i], lambda_k2[i])
        d_skip_lanes = jnp.repeat(d_skip[i], SSM_HEADDIM).reshape(1, SSM_INNER)

        xall = _ffn(xall, t, mod3, row_fn, 0, ffn1_norm[i], w_gu1, w_d1, tm, tf)

        p, dt_raw = _inproj(xall, mod3, row_fn, rope_fn, mix_norm[i], w_p, w_dt, cos_tab, sin_tab, gmat, qkw, tm)
        o_attn = _attention(p, t_lat, 0, l, tq, [(0, l), (t_lat, lc)], lam_init, lam_params, attn_subln[i])
        xbc = _conv(p, conv_w[i], conv_b[i], t_lat, l, lc)
        y2 = _ssd(xbc, dt_raw, dt_bias[i], a_log[i], expand, nb, t_lat, l, lc)
        n_rows = t_lat if last else t
        if not last:
            o_attn_c = _attention(p, t_ctx, t_lat, lc, min(tq, lc), [(t_lat, lc)], lam_init, lam_params,
                                  attn_subln[i])
            o_attn = jnp.concatenate([o_attn, o_attn_c], axis=0)
        o_ssm = _ssm_finish(y2, xbc, p, n_rows, d_skip_lanes, ssm_norm[i], tm)
        u = _merge(o_attn, o_ssm, p, n_rows, w_ba, w_bs, tm)
        xall = _outproj(xall, u, n_rows, mod3, row_fn, w_o, tm)

        xall = _ffn(xall, n_rows, mod3, row_fn, 2, ffn2_norm[i], w_gu2, w_d2, tm, tf)

    return xall[:t_lat].reshape(nb, l, d)
```

```python
import functools
import math

import jax
import jax.numpy as jnp
from jax import lax
from jax.experimental import pallas as pl
from jax.experimental.pallas import tpu as pltpu

f32 = jnp.float32
bf16 = jnp.bfloat16

EPS = 1e-6
N_MOD = 9
FFN_RESIDUAL = 0.5
GRID_W = 64
ROPE_BASE = 10000.0

ATT_HEADS = 8
ATT_DH = 64
ATT_VD = 2 * ATT_DH
ATT_WIDTH = ATT_HEADS * ATT_VD

SSM_INNER = 2048
SSM_HEADDIM = 64
SSM_HEADS = SSM_INNER // SSM_HEADDIM
SSM_GROUPS = 4
SSM_HPG = SSM_HEADS // SSM_GROUPS
SSM_STATE = 128
SSM_CHUNK = 128
CONV_W = 5
CONV_DIM = SSM_INNER + 2 * SSM_GROUPS * SSM_STATE
GROUP_W = SSM_HPG * SSM_HEADDIM

LANE = 128
COND_ROWS = 16
HALO = 16
NEG_BIG = -1e30
LOG2E = math.log2(math.e)
QK_NORM_W = 512

P_Q, P_K, P_V = 0, ATT_WIDTH, 2 * ATT_WIDTH
P_XBC = 3 * ATT_WIDTH
P_Z = P_XBC + CONV_DIM
P_GA = P_Z + SSM_INNER

VMEM_LIMIT = 58 * 1024 * 1024


def _cparams(sem):
    return pltpu.CompilerParams(dimension_semantics=sem, vmem_limit_bytes=VMEM_LIMIT)


def _resident(shape):
    return pl.BlockSpec(shape, lambda *_: (0,) * len(shape), pipeline_mode=pl.Buffered(1))


def _silu(v):
    return v * jax.nn.sigmoid(v)


def _mod_norm(x, nw, shift, scale):
    y = x * lax.rsqrt(jnp.mean(x * x, axis=-1, keepdims=True) + EPS)
    return (y * nw) * (1.0 + scale) + shift


def _adaln_kernel(c_ref, w_ref, b_ref, o_ref):
    c = c_ref[...]
    s = _silu(c).astype(bf16)
    o_ref[...] = jnp.dot(s, w_ref[...].astype(bf16), preferred_element_type=f32) + b_ref[...]


def _adaln(conds, ada_w, ada_b):
    depth, d, n = ada_w.shape
    tn = 1024
    return pl.pallas_call(
        _adaln_kernel,
        grid=(depth, n // tn),
        in_specs=[
            pl.BlockSpec((COND_ROWS, d), lambda l, j: (0, 0)),
            pl.BlockSpec((None, d, tn), lambda l, j: (l, 0, j)),
            pl.BlockSpec((None, 1, tn), lambda l, j: (l, 0, j)),
        ],
        out_specs=pl.BlockSpec((None, COND_ROWS, tn), lambda l, j: (l, 0, j)),
        out_shape=jax.ShapeDtypeStruct((depth, COND_ROWS, n), f32),
        compiler_params=_cparams(("parallel", "parallel")),
    )(conds, ada_w, ada_b.reshape(depth, 1, n))


def _ffn_kernel(x_ref, sh_ref, sc_ref, gt_ref, nw_ref, wg_ref, wv_ref, wd_ref, o_ref, xn_ref):
    j = pl.program_id(1)

    @pl.when(j == 0)
    def _():
        x = x_ref[...]
        xn_ref[...] = _mod_norm(x, nw_ref[...], sh_ref[...], sc_ref[...]).astype(bf16)
        o_ref[...] = x

    xn = xn_ref[...]
    g = jnp.dot(xn, wg_ref[...], preferred_element_type=f32)
    v = jnp.dot(xn, wv_ref[...], preferred_element_type=f32)
    a = (_silu(g) * v).astype(bf16)
    half = o_ref.shape[1] // 2
    for c in range(2):
        cols = slice(c * half, (c + 1) * half)
        o_ref[:, cols] += (FFN_RESIDUAL * gt_ref[:, cols]) * jnp.dot(a, wd_ref[:, cols], preferred_element_type=f32)


def _mod_spec(d, row_fn, col):
    return pl.BlockSpec((None, 1, d), lambda i, *_: (row_fn(i), 0, col))


def _ffn(xall, n_rows, mod3, row_fn, slot, norm_w, w_gu, w_down, tm, tf):
    d = xall.shape[1]
    ff = w_down.shape[0]
    nf = ff // tf
    return pl.pallas_call(
        _ffn_kernel,
        grid=(n_rows // tm, nf),
        in_specs=[
            pl.BlockSpec((tm, d), lambda i, j: (i, 0), pipeline_mode=pl.Buffered(1)),
            _mod_spec(d, row_fn, 3 * slot),
            _mod_spec(d, row_fn, 3 * slot + 1),
            _mod_spec(d, row_fn, 3 * slot + 2),
            pl.BlockSpec((1, d), lambda i, j: (0, 0)),
            pl.BlockSpec((d, tf), lambda i, j: (0, j)),
            pl.BlockSpec((d, tf), lambda i, j: (0, nf + j)),
            pl.BlockSpec((tf, d), lambda i, j: (j, 0)),
        ],
        out_specs=pl.BlockSpec((tm, d), lambda i, j: (i, 0)),
        out_shape=jax.ShapeDtypeStruct((n_rows, d), f32),
        scratch_shapes=[pltpu.VMEM((tm, d), bf16)],
        compiler_params=_cparams(("parallel", "arbitrary")),
    )(xall, mod3, mod3, mod3, norm_w.reshape(1, d), w_gu, w_gu, w_down)


def _inproj_kernel(x_ref, sh_ref, sc_ref, nw_ref, w_ref, wdt_ref, cos_ref, sin_ref, g_ref, qkw_ref,
                   p_ref, dt_ref, xn_ref, *, n_qk, n_plain_end):
    j = pl.program_id(1)

    @pl.when(j == 0)
    def _():
        xn = _mod_norm(x_ref[...], nw_ref[...], sh_ref[...], sc_ref[...]).astype(bf16)
        xn_ref[...] = xn
        dt_ref[...] = jnp.dot(xn, wdt_ref[...], preferred_element_type=f32)

    y = jnp.dot(xn_ref[...], w_ref[...], preferred_element_type=f32)

    @pl.when(j < n_qk)
    def _():
        cos, sin = cos_ref[...], sin_ref[...]
        for c in range(y.shape[1] // QK_NORM_W):
            yc = y[:, c * QK_NORM_W:(c + 1) * QK_NORM_W]
            ss = jnp.dot((yc * yc).astype(bf16), g_ref[...], preferred_element_type=f32)
            yn = yc * lax.rsqrt(ss * (1.0 / ATT_DH) + EPS) * qkw_ref[:, c * QK_NORM_W:(c + 1) * QK_NORM_W]
            for h in range(QK_NORM_W // LANE):
                sl = yn[:, h * LANE:(h + 1) * LANE]
                lo = c * QK_NORM_W + h * LANE
                p_ref[:, lo:lo + LANE] = (sl * cos + pltpu.roll(sl, LANE // 2, 1) * sin).astype(bf16)

    @pl.when(jnp.logical_and(j >= n_qk, j < n_plain_end))
    def _():
        p_ref[...] = y.astype(bf16)

    @pl.when(j >= n_plain_end)
    def _():
        p_ref[...] = jax.nn.sigmoid(y).astype(bf16)


def _inproj(xall, mod3, row_fn, rope_fn, mix_norm, w_p, w_dt, cos_tab, sin_tab, gmat, qkw, tm, tn):
    t, d = xall.shape
    n = w_p.shape[1]
    n_qk = 2 * ATT_WIDTH // tn
    kern = functools.partial(_inproj_kernel, n_qk=n_qk, n_plain_end=P_GA // tn)
    return pl.pallas_call(
        kern,
        grid=(t // tm, n // tn),
        in_specs=[
            pl.BlockSpec((tm, d), lambda i, j: (i, 0)),
            _mod_spec(d, row_fn, 3),
            _mod_spec(d, row_fn, 4),
            pl.BlockSpec((1, d), lambda i, j: (0, 0)),
            pl.BlockSpec((d, tn), lambda i, j: (0, j)),
            pl.BlockSpec((d, 2 * LANE), lambda i, j: (0, 0)),
            pl.BlockSpec((tm, LANE), lambda i, j: (rope_fn(i), 0)),
            pl.BlockSpec((tm, LANE), lambda i, j: (rope_fn(i), 0)),
            pl.BlockSpec((QK_NORM_W, QK_NORM_W), lambda i, j: (0, 0)),
            pl.BlockSpec((None, 1, tn), lambda i, j: (jnp.minimum(j // (n_qk // 2), 1), 0, 0)),
        ],
        out_specs=[
            pl.BlockSpec((tm, tn), lambda i, j: (i, j)),
            pl.BlockSpec((tm, 2 * LANE), lambda i, j: (i, 0)),
        ],
        out_shape=[jax.ShapeDtypeStruct((t, n), bf16), jax.ShapeDtypeStruct((t, 2 * LANE), f32)],
        scratch_shapes=[pltpu.VMEM((tm, d), bf16)],
        compiler_params=_cparams(("parallel", "arbitrary")),
    )(xall, mod3, mod3, mix_norm.reshape(1, d), w_p, w_dt, cos_tab, sin_tab, gmat, qkw)


def _attn_kernel(*refs, n_seg, lam_init):
    q_ref = refs[0]
    kv_refs = refs[1:1 + 2 * n_seg]
    lq1_ref, lk1_ref, lq2_ref, lk2_ref, sub_ref, o_ref = refs[1 + 2 * n_seg:]

    lam = (jnp.exp(jnp.sum(lq1_ref[...] * lk1_ref[...], axis=-1, keepdims=True))
           - jnp.exp(jnp.sum(lq2_ref[...] * lk2_ref[...], axis=-1, keepdims=True)) + lam_init)

    q = q_ref[...]
    comp = (lax.broadcasted_iota(jnp.int32, q.shape, 1) // (ATT_DH // 2)) % 2
    qc = [jnp.where(comp == 0, q, jnp.zeros_like(q)), jnp.where(comp == 1, q, jnp.zeros_like(q))]

    s = [[lax.dot_general(qc[c], kv_refs[2 * g][...], (((1,), (1,)), ((), ())), preferred_element_type=f32)
          for g in range(n_seg)] for c in range(2)]
    coef = []
    e = []
    for c in range(2):
        m = functools.reduce(jnp.maximum, [jnp.max(sg, axis=-1, keepdims=True) for sg in s[c]])
        ec = [jnp.exp2(sg - m) for sg in s[c]]
        den = functools.reduce(lambda u, v: u + v, [jnp.sum(eg, axis=-1, keepdims=True) for eg in ec])
        e.append(ec)
        coef.append(1.0 / den)
    coef[1] = coef[1] * lam
    o = None
    for g in range(n_seg):
        a = (e[0][g] * coef[0] - e[1][g] * coef[1]).astype(bf16)
        og = jnp.dot(a, kv_refs[2 * g + 1][...], preferred_element_type=f32)
        o = og if o is None else o + og
    on = o * lax.rsqrt(jnp.mean(o * o, axis=-1, keepdims=True) + EPS) * sub_ref[...] * (1.0 - lam_init)
    o_ref[...] = on.astype(bf16)


def _attention(p, n_q_rows, q_row0, lq, tq, segs, lam_init, lam_params, subln):
    nb = n_q_rows // lq
    nqt = lq // tq
    in_specs = [pl.BlockSpec((tq, LANE), lambda b, h, i: (q_row0 // tq + b * nqt + i, P_Q // LANE + h))]
    args = [p]
    for row0, ls in segs:
        in_specs.append(pl.BlockSpec((ls, LANE), functools.partial(
            lambda b, h, i, r0, col: (r0 + b, col + h), r0=row0 // ls, col=P_K // LANE)))
        in_specs.append(pl.BlockSpec((ls, LANE), functools.partial(
            lambda b, h, i, r0, col: (r0 + b, col + h), r0=row0 // ls, col=P_V // LANE)))
        args += [p, p]
    for lp in lam_params:
        in_specs.append(pl.BlockSpec((1, ATT_DH), lambda b, h, i: (0, 0)))
        args.append(lp.reshape(1, ATT_DH))
    in_specs.append(pl.BlockSpec((1, ATT_VD), lambda b, h, i: (0, 0)))
    args.append(subln.reshape(1, ATT_VD))
    kern = functools.partial(_attn_kernel, n_seg=len(segs), lam_init=lam_init)
    return pl.pallas_call(
        kern,
        grid=(nb, ATT_HEADS, nqt),
        in_specs=in_specs,
        out_specs=pl.BlockSpec((tq, LANE), lambda b, h, i: (b * nqt + i, h)),
        out_shape=jax.ShapeDtypeStruct((n_q_rows, ATT_WIDTH), bf16),
        compiler_params=_cparams(("parallel", "parallel", "arbitrary")),
    )(*args)


def _conv_kernel(x_ref, prev_ref, next_ref, w_ref, b_ref, o_ref, pad_ref, *, rows, n_lat_blocks, lat_per_seq,
                 ctx_per_seq):
    i = pl.program_id(0)
    is_lat = i < n_lat_blocks
    per = jnp.where(is_lat, lat_per_seq, ctx_per_seq)
    pos = jnp.where(is_lat, i % lat_per_seq, (i - n_lat_blocks) % ctx_per_seq)
    keep_prev = (pos != 0).astype(f32)
    keep_next = (pos != per - 1).astype(f32)
    pad_ref[0:HALO, :] = prev_ref[...].astype(f32) * keep_prev
    pad_ref[HALO:HALO + rows, :] = x_ref[...].astype(f32)
    pad_ref[HALO + rows:2 * HALO + rows, :] = next_ref[...].astype(f32) * keep_next
    acc = jnp.zeros((rows, x_ref.shape[1]), f32) + b_ref[...]
    for k in range(CONV_W):
        acc = acc + w_ref[k:k + 1, :] * pad_ref[pl.ds(HALO - CONV_W // 2 + k, rows), :]
    o_ref[...] = _silu(acc).astype(bf16)


def _conv(p, conv_w, conv_b, t_lat, l, lc):
    t = p.shape[0]
    rows = min(math.gcd(l, lc), 256)
    tc = 1024
    hb = rows // HALO
    c0 = P_XBC // tc
    kern = functools.partial(_conv_kernel, rows=rows, n_lat_blocks=t_lat // rows, lat_per_seq=l // rows,
                             ctx_per_seq=lc // rows)
    return pl.pallas_call(
        kern,
        grid=(t // rows, CONV_DIM // tc),
        in_specs=[
            pl.BlockSpec((rows, tc), lambda i, j: (i, c0 + j)),
            pl.BlockSpec((HALO, tc), lambda i, j: (jnp.maximum(i * hb - 1, 0), c0 + j)),
            pl.BlockSpec((HALO, tc), lambda i, j: (jnp.minimum((i + 1) * hb, t // HALO - 1), c0 + j)),
            pl.BlockSpec((CONV_W, tc), lambda i, j: (0, j)),
            pl.BlockSpec((1, tc), lambda i, j: (0, j)),
        ],
        out_specs=pl.BlockSpec((rows, tc), lambda i, j: (i, j)),
        out_shape=jax.ShapeDtypeStruct((t, CONV_DIM), bf16),
        scratch_shapes=[pltpu.VMEM((rows + 2 * HALO, tc), f32)],
        compiler_params=_cparams(("parallel", "parallel")),
    )(p, p, p, conv_w, conv_b.reshape(1, CONV_DIM))


def _ssd_kernel(xs_ref, b_ref, c_ref, dt_ref, bias_ref, alog_ref, y_ref, h_ref):
    d = pl.program_id(0)
    t = pl.program_id(2)
    ck = SSM_CHUNK
    hi = lax.Precision.HIGHEST

    @pl.when(t == 0)
    def _():
        h_ref[...] = jnp.zeros_like(h_ref)

    dtv = jax.nn.softplus(dt_ref[...] + bias_ref[...])
    adt = dtv * (-jnp.exp(alog_ref[...]))
    li = lax.broadcasted_iota(jnp.int32, (ck, ck), 0)
    si = lax.broadcasted_iota(jnp.int32, (ck, ck), 1)
    sgn = jnp.where(d == 0, 1, -1)
    mask = (si - li) * sgn <= 0
    mask_t = (li - si) * sgn <= 0
    acum = jnp.dot(mask.astype(f32), adt, precision=hi, preferred_element_type=f32)
    adt_t = adt.T
    dtv_t = dtv.T
    acum_t = jnp.dot(adt_t, mask_t.astype(f32), precision=hi, preferred_element_type=f32)
    tot_t = jnp.sum(adt_t, axis=1, keepdims=True)
    w_t = jnp.exp(tot_t - acum_t) * dtv_t
    e_col = jnp.exp(acum)
    dec_row = jnp.exp(jnp.sum(adt, axis=0, keepdims=True))

    lane_head = lax.broadcasted_iota(jnp.int32, (ck, GROUP_W), 1) // SSM_HEADDIM
    lane_head1 = lane_head[0:1, :]
    for g in range(SSM_GROUPS):
        bg = b_ref[:, g * SSM_STATE:(g + 1) * SSM_STATE]
        cg = c_ref[:, g * SSM_STATE:(g + 1) * SSM_STATE]
        cb = lax.dot_general(cg, bg, (((1,), (1,)), ((), ())), preferred_element_type=f32)
        bg_t = bg.astype(f32).T
        cg_f = cg.astype(f32)
        xg = xs_ref[:, g * GROUP_W:(g + 1) * GROUP_W]
        hg = h_ref[g]
        xm = jnp.concatenate([jnp.where(lane_head == r, xg, jnp.zeros_like(xg)) for r in range(SSM_HPG)], axis=0)
        hm = jnp.concatenate([jnp.where(lane_head == r, hg, 0.0).astype(bf16) for r in range(SSM_HPG)], axis=0)
        lhs_diag, lhs_off, lhs_st = [], [], []
        dec_g = jnp.zeros((1, GROUP_W), f32)
        for r in range(SSM_HPG):
            hh = g * SSM_HPG + r
            seg = jnp.broadcast_to(acum[:, hh:hh + 1], (ck, ck)) - acum_t[hh:hh + 1, :]
            decay = jnp.exp(jnp.where(mask, seg, NEG_BIG))
            lhs_diag.append((cb * decay * dtv_t[hh:hh + 1, :]).astype(bf16))
            lhs_off.append((cg_f * jnp.broadcast_to(e_col[:, hh:hh + 1], (ck, SSM_STATE))).astype(bf16))
            lhs_st.append((bg_t * w_t[hh:hh + 1, :]).astype(bf16))
            dec_g = jnp.where(lane_head1 == r, jnp.broadcast_to(dec_row[:, hh:hh + 1], (1, GROUP_W)), dec_g)
        y = jnp.dot(jnp.concatenate(lhs_diag + lhs_off, axis=1), jnp.concatenate([xm, hm], axis=0),
                    preferred_element_type=f32)
        y_ref[:, g * GROUP_W:(g + 1) * GROUP_W] = y.astype(bf16)
        st = jnp.dot(jnp.concatenate(lhs_st, axis=1), xm, preferred_element_type=f32)
        h_ref[g] = hg * dec_g + st


def _ssd(xbc, dt_raw, dt_bias, a_log, nb, t_lat, l, lc):
    t = xbc.shape[0]
    ck = SSM_CHUNK
    nck, nlk = lc // ck, l // ck

    def rowblk(d, b, s):
        tc = jnp.where(d == 0, s, nck - 1 - s)
        tl = jnp.where(d == 0, s - nck, nlk - 1 - (s - nck))
        return jnp.where(s < nck, t_lat // ck + b * nck + tc, b * nlk + tl)

    bc0 = SSM_INNER // (SSM_GROUPS * SSM_STATE)
    pad = lambda v: jnp.pad(v.astype(f32), ((0, 0), (0, LANE - SSM_HEADS))).reshape(2, 1, LANE)
    return pl.pallas_call(
        _ssd_kernel,
        grid=(2, nb, nck + nlk),
        in_specs=[
            pl.BlockSpec((ck, SSM_INNER), lambda d, b, s: (rowblk(d, b, s), 0)),
            pl.BlockSpec((ck, SSM_GROUPS * SSM_STATE), lambda d, b, s: (rowblk(d, b, s), bc0)),
            pl.BlockSpec((ck, SSM_GROUPS * SSM_STATE), lambda d, b, s: (rowblk(d, b, s), bc0 + 1)),
            pl.BlockSpec((ck, LANE), lambda d, b, s: (rowblk(d, b, s), d)),
            pl.BlockSpec((None, 1, LANE), lambda d, b, s: (d, 0, 0)),
            pl.BlockSpec((None, 1, LANE), lambda d, b, s: (d, 0, 0)),
        ],
        out_specs=pl.BlockSpec((None, ck, SSM_INNER), lambda d, b, s: (d, rowblk(d, b, s), 0)),
        out_shape=jax.ShapeDtypeStruct((2, t, SSM_INNER), bf16),
        scratch_shapes=[pltpu.VMEM((SSM_GROUPS, SSM_STATE, GROUP_W), f32)],
        compiler_params=_cparams(("parallel", "parallel", "arbitrary")),
    )(xbc, xbc, xbc, dt_raw, pad(dt_bias), pad(a_log))


def _post_kernel(oal_ref, oac_ref, yf_ref, yb_ref, xs_ref, z_ref, ga_ref, gs_ref, x_ref, gt_ref, dsk_ref, nw_ref,
                 wa_ref, ws_ref, wo_ref, o_ref, *, n_lat_tiles):
    i = pl.program_id(0)
    y = yf_ref[...].astype(f32) + yb_ref[...].astype(f32) + dsk_ref[...] * xs_ref[...].astype(f32)
    gz = y * _silu(z_ref[...].astype(f32))
    gw = SSM_INNER // SSM_GROUPS
    parts = []
    for g in range(SSM_GROUPS):
        v = gz[:, g * gw:(g + 1) * gw]
        vn = v * lax.rsqrt(jnp.mean(v * v, axis=-1, keepdims=True) + EPS) * nw_ref[:, g * gw:(g + 1) * gw]
        parts.append(vn.astype(bf16))
    o_ssm = jnp.concatenate(parts, axis=1)
    o_attn = jnp.where(i < n_lat_tiles, oal_ref[...], oac_ref[...])
    ua = jnp.dot(o_attn, wa_ref[...], preferred_element_type=f32)
    us = jnp.dot(o_ssm, ws_ref[...], preferred_element_type=f32)
    u = (ga_ref[...].astype(f32) * ua + gs_ref[...].astype(f32) * us).astype(bf16)
    o_ref[...] = x_ref[...] + gt_ref[...] * jnp.dot(u, wo_ref[...], preferred_element_type=f32)


def _post(o_attn_lat, o_attn_ctx, n_lat_tiles, y2, xbc, p, xall, n_rows, mod3, row_fn, d_skip_lanes, ssm_norm,
          w_ba, w_bs, w_o, tp):
    d = w_o.shape[1]
    si = SSM_INNER
    kern = functools.partial(_post_kernel, n_lat_tiles=n_lat_tiles)
    return pl.pallas_call(
        kern,
        grid=(n_rows // tp,),
        in_specs=[
            pl.BlockSpec((tp, ATT_WIDTH), lambda i: (jnp.minimum(i, n_lat_tiles - 1), 0)),
            pl.BlockSpec((tp, ATT_WIDTH), lambda i: (jnp.maximum(i - n_lat_tiles, 0), 0)),
            pl.BlockSpec((None, tp, si), lambda i: (0, i, 0)),
            pl.BlockSpec((None, tp, si), lambda i: (1, i, 0)),
            pl.BlockSpec((tp, si), lambda i: (i, 0)),
            pl.BlockSpec((tp, si), lambda i: (i, P_Z // si)),
            pl.BlockSpec((tp, d), lambda i: (i, P_GA // d)),
            pl.BlockSpec((tp, d), lambda i: (i, P_GA // d + 1)),
            pl.BlockSpec((tp, d), lambda i: (i, 0)),
            _mod_spec(d, row_fn, 5),
            pl.BlockSpec((1, si), lambda i: (0, 0)),
            pl.BlockSpec((1, si), lambda i: (0, 0)),
            _resident((ATT_WIDTH, d)),
            _resident((si, d)),
            _resident((d, d)),
        ],
        out_specs=pl.BlockSpec((tp, d), lambda i: (i, 0)),
        out_shape=jax.ShapeDtypeStruct((n_rows, d), f32),
        compiler_params=_cparams(("parallel",)),
    )(o_attn_lat, o_attn_ctx, y2, y2, xbc, p, p, p, xall, mod3, d_skip_lanes, ssm_norm.reshape(1, si),
      w_ba, w_bs, w_o)


def _qk_layout():
    lane = jnp.arange(LANE)
    par, comp, freq = lane // 64, (lane // 32) % 2, lane % 32
    within = comp * ATT_DH + 2 * freq + par
    return (jnp.arange(ATT_HEADS)[:, None] * LANE + within[None, :]).reshape(-1), (2 * freq + par)


def _rope_tables(l, tm):
    n_freq = ATT_DH // 4
    rows = l // GRID_W
    inv = ROPE_BASE ** (-jnp.arange(n_freq, dtype=f32) / n_freq)
    row = jnp.repeat(jnp.arange(rows, dtype=f32), GRID_W)
    col = jnp.tile(jnp.arange(GRID_W, dtype=f32), rows)
    ang = jnp.concatenate([row[:, None] * inv, col[:, None] * inv], axis=-1)
    cos, sin = jnp.cos(ang), jnp.sin(ang)
    cos_l = jnp.tile(cos, (1, LANE // (ATT_DH // 2)))
    sin_l = jnp.concatenate([-sin, -sin, sin, sin], axis=-1)
    cos_tab = jnp.concatenate([cos_l, jnp.ones((tm, LANE), f32)], axis=0)
    sin_tab = jnp.concatenate([sin_l, jnp.zeros((tm, LANE), f32)], axis=0)
    return cos_tab, sin_tab


def _tiles(l, lc, nb):
    tm = min(1024, l)
    tp = min(256, l)
    tq = min(512, l)
    assert l % tm == 0 and (nb * lc) % tm == 0 and l % tp == 0 and (nb * lc) % tp == 0
    return tm, tp, tq


def kernel(x, c, ctx, c_ctx, ada_w, ada_b, ffn1_norm, ffn1_w_gu, ffn1_w_down, mix_norm, w_in, q_norm, k_norm,
           lambda_q1, lambda_k1, lambda_q2, lambda_k2, attn_subln, conv_w, conv_b, dt_bias, a_log, d_skip,
           ssm_norm, w_branch_attn, w_branch_ssm, w_out, ffn2_norm, ffn2_w_gu, ffn2_w_down):
    nb, l, d = x.shape
    lc = ctx.shape[1]
    depth = ada_w.shape[0]
    t_lat, t_ctx = nb * l, nb * lc
    t = t_lat + t_ctx
    tm, tp, tq = _tiles(l, lc, nb)
    tf = 512
    tn = 1024
    assert nb < COND_ROWS and l % GRID_W == 0

    xall = jnp.concatenate([x.reshape(t_lat, d), ctx.reshape(t_ctx, d)], axis=0)
    conds = jnp.zeros((COND_ROWS, d), f32).at[:nb].set(c).at[nb].set(c_ctx)
    mod = _adaln(conds, ada_w, ada_b)
    mod3 = mod.reshape(depth * COND_ROWS, 1, N_MOD * d)

    perm, norm_idx = _qk_layout()
    cos_tab, sin_tab = _rope_tables(l, tm)
    rope_fn = lambda i: jnp.where(i < t_lat // tm, i % (l // tm), l // tm)
    lane_n = jnp.arange(QK_NORM_W)
    grp = (lane_n // LANE) * 2 + (lane_n // 32) % 2
    gmat = (grp[:, None] == grp[None, :]).astype(bf16)

    s0 = 3 * ATT_WIDTH
    dt0 = s0 + SSM_INNER + CONV_DIM
    g0 = dt0 + 2 * SSM_HEADS

    def cond_row(tile_rows, base):
        return lambda ti: base + jnp.where(ti < t_lat // tile_rows, ti // (l // tile_rows), nb)

    for i in range(depth):
        last = i == depth - 1
        row_m = cond_row(tm, i * COND_ROWS)
        row_p = cond_row(tp, i * COND_ROWS)
        lam_init = 0.8 - 0.6 * math.exp(-0.3 * i)

        w = w_in[i]
        w_p = jnp.concatenate([
            w[:, perm], w[:, ATT_WIDTH + perm], w[:, 2 * ATT_WIDTH:s0],
            w[:, s0 + SSM_INNER:dt0], w[:, s0:s0 + SSM_INNER], w[:, g0:]], axis=1).astype(bf16)
        zpad = jnp.zeros((d, LANE - SSM_HEADS), f32)
        w_dt = jnp.concatenate([w[:, dt0:dt0 + SSM_HEADS], zpad, w[:, dt0 + SSM_HEADS:g0], zpad],
                               axis=1).astype(bf16)
        qkw = jnp.stack([jnp.tile(q_norm[i][norm_idx] * (LOG2E * ATT_DH ** -0.5), ATT_HEADS),
                         jnp.tile(k_norm[i][norm_idx], ATT_HEADS)]).reshape(2, 1, ATT_WIDTH)
        w_gu1, w_d1 = ffn1_w_gu[i].astype(bf16), ffn1_w_down[i].astype(bf16)
        w_gu2, w_d2 = ffn2_w_gu[i].astype(bf16), ffn2_w_down[i].astype(bf16)
        w_ba, w_bs, w_o = w_branch_attn[i].astype(bf16), w_branch_ssm[i].astype(bf16), w_out[i].astype(bf16)
        lam_params = (lambda_q1[i], lambda_k1[i], lambda_q2[i], lambda_k2[i])
        d_skip_lanes = jnp.repeat(d_skip[i], SSM_HEADDIM).reshape(1, SSM_INNER)

        xall = _ffn(xall, t, mod3, row_m, 0, ffn1_norm[i], w_gu1, w_d1, tm, tf)

        p, dt_raw = _inproj(xall, mod3, row_m, rope_fn, mix_norm[i], w_p, w_dt, cos_tab, sin_tab, gmat, qkw,
                            tm, tn)
        o_attn = _attention(p, t_lat, 0, l, tq, [(0, l), (t_lat, lc)], lam_init, lam_params, attn_subln[i])
        xbc = _conv(p, conv_w[i], conv_b[i], t_lat, l, lc)
        y2 = _ssd(xbc, dt_raw, dt_bias[i], a_log[i], nb, t_lat, l, lc)
        if last:
            n_rows, o_attn_c = t_lat, o_attn
        else:
            n_rows = t
            o_attn_c = _attention(p, t_ctx, t_lat, lc, min(tq, lc), [(t_lat, lc)], lam_init, lam_params,
                                  attn_subln[i])
        xall = _post(o_attn, o_attn_c, t_lat // tp, y2, xbc, p, xall, n_rows, mod3, row_p, d_skip_lanes,
                     ssm_norm[i], w_ba, w_bs, w_o, tp)

        xall = _ffn(xall, n_rows, mod3, row_m, 2, ffn2_norm[i], w_gu2, w_d2, tm, tf)

    return xall[:t_lat].reshape(nb, l, d)
```

```python
import functools
import math

import jax
import jax.numpy as jnp
from jax import lax
from jax.experimental import pallas as pl
from jax.experimental.pallas import tpu as pltpu

f32 = jnp.float32
bf16 = jnp.bfloat16

EPS = 1e-6
N_MOD = 9
FFN_RESIDUAL = 0.5
GRID_W = 64
ROPE_BASE = 10000.0

ATT_HEADS = 8
ATT_DH = 64
ATT_VD = 2 * ATT_DH
ATT_WIDTH = ATT_HEADS * ATT_VD

SSM_INNER = 2048
SSM_HEADDIM = 64
SSM_HEADS = SSM_INNER // SSM_HEADDIM
SSM_GROUPS = 4
SSM_HPG = SSM_HEADS // SSM_GROUPS
SSM_STATE = 128
SSM_CHUNK = 128
CONV_W = 5
CONV_DIM = SSM_INNER + 2 * SSM_GROUPS * SSM_STATE
GROUP_W = SSM_HPG * SSM_HEADDIM

LANE = 128
COND_ROWS = 16
HALO = 16
NEG_BIG = -1e30
LOG2E = math.log2(math.e)
QK_NORM_W = 512

P_Q, P_K, P_V = 0, ATT_WIDTH, 2 * ATT_WIDTH
P_XBC = 3 * ATT_WIDTH
P_Z = P_XBC + CONV_DIM
P_GA = P_Z + SSM_INNER

VMEM_LIMIT = 58 * 1024 * 1024


def _cparams(sem):
    return pltpu.CompilerParams(dimension_semantics=sem, vmem_limit_bytes=VMEM_LIMIT)


def _resident(shape):
    return pl.BlockSpec(shape, lambda *_: (0,) * len(shape), pipeline_mode=pl.Buffered(1))


def _silu(v):
    return v * jax.nn.sigmoid(v)


def _mod_norm(x, nw, shift, scale):
    y = x * lax.rsqrt(jnp.mean(x * x, axis=-1, keepdims=True) + EPS)
    return (y * nw) * (1.0 + scale) + shift


def _adaln_kernel(c_ref, w_ref, b_ref, o_ref):
    c = c_ref[...]
    s = _silu(c).astype(bf16)
    o_ref[...] = jnp.dot(s, w_ref[...].astype(bf16), preferred_element_type=f32) + b_ref[...]


def _adaln(conds, ada_w, ada_b):
    depth, d, n = ada_w.shape
    tn = 1024
    return pl.pallas_call(
        _adaln_kernel,
        grid=(depth, n // tn),
        in_specs=[
            pl.BlockSpec((COND_ROWS, d), lambda l, j: (0, 0)),
            pl.BlockSpec((None, d, tn), lambda l, j: (l, 0, j)),
            pl.BlockSpec((None, 1, tn), lambda l, j: (l, 0, j)),
        ],
        out_specs=pl.BlockSpec((None, COND_ROWS, tn), lambda l, j: (l, 0, j)),
        out_shape=jax.ShapeDtypeStruct((depth, COND_ROWS, n), f32),
        compiler_params=_cparams(("parallel", "parallel")),
    )(conds, ada_w, ada_b.reshape(depth, 1, n))


def _ffn_kernel(x_ref, sh_ref, sc_ref, gt_ref, nw_ref, wg_ref, wv_ref, wd_ref, o_ref, xn_ref, acc_ref):
    j = pl.program_id(1)

    @pl.when(j == 0)
    def _():
        xn_ref[...] = _mod_norm(x_ref[...], nw_ref[...], sh_ref[...], sc_ref[...]).astype(bf16)
        acc_ref[...] = jnp.zeros_like(acc_ref)

    xn = xn_ref[...]
    g = jnp.dot(xn, wg_ref[...], preferred_element_type=f32)
    v = jnp.dot(xn, wv_ref[...], preferred_element_type=f32)
    a = (_silu(g) * v).astype(bf16)
    acc_ref[...] += jnp.dot(a, wd_ref[...], preferred_element_type=f32)

    @pl.when(j == pl.num_programs(1) - 1)
    def _():
        o_ref[...] = x_ref[...] + (FFN_RESIDUAL * gt_ref[...]) * acc_ref[...]


def _mod_spec(d, row_fn, col):
    return pl.BlockSpec((None, 1, d), lambda i, *_: (row_fn(i), 0, col))


def _ffn(xall, n_rows, mod3, row_fn, slot, norm_w, w_gu, w_down, tm, tf):
    d = xall.shape[1]
    ff = w_down.shape[0]
    nf = ff // tf
    return pl.pallas_call(
        _ffn_kernel,
        grid=(n_rows // tm, nf),
        in_specs=[
            pl.BlockSpec((tm, d), lambda i, j: (i, 0)),
            _mod_spec(d, row_fn, 3 * slot),
            _mod_spec(d, row_fn, 3 * slot + 1),
            _mod_spec(d, row_fn, 3 * slot + 2),
            pl.BlockSpec((1, d), lambda i, j: (0, 0)),
            pl.BlockSpec((d, tf), lambda i, j: (0, j)),
            pl.BlockSpec((d, tf), lambda i, j: (0, nf + j)),
            pl.BlockSpec((tf, d), lambda i, j: (j, 0)),
        ],
        out_specs=pl.BlockSpec((tm, d), lambda i, j: (i, 0)),
        out_shape=jax.ShapeDtypeStruct((n_rows, d), f32),
        scratch_shapes=[pltpu.VMEM((tm, d), bf16), pltpu.VMEM((tm, d), f32)],
        compiler_params=_cparams(("parallel", "arbitrary")),
    )(xall, mod3, mod3, mod3, norm_w.reshape(1, d), w_gu, w_gu, w_down)


def _inproj_kernel(x_ref, sh_ref, sc_ref, nw_ref, w_ref, wdt_ref, cos_ref, sin_ref, g_ref, qkw_ref,
                   p_ref, dt_ref, xn_ref, *, n_qk, n_plain_end):
    j = pl.program_id(1)

    @pl.when(j == 0)
    def _():
        xn = _mod_norm(x_ref[...], nw_ref[...], sh_ref[...], sc_ref[...]).astype(bf16)
        xn_ref[...] = xn
        dt_ref[...] = jnp.dot(xn, wdt_ref[...], preferred_element_type=f32)

    y = jnp.dot(xn_ref[...], w_ref[...], preferred_element_type=f32)

    @pl.when(j < n_qk)
    def _():
        cos, sin = cos_ref[...], sin_ref[...]
        for c in range(y.shape[1] // QK_NORM_W):
            yc = y[:, c * QK_NORM_W:(c + 1) * QK_NORM_W]
            ss = jnp.dot((yc * yc).astype(bf16), g_ref[...], preferred_element_type=f32)
            yn = yc * lax.rsqrt(ss * (1.0 / ATT_DH) + EPS) * qkw_ref[:, c * QK_NORM_W:(c + 1) * QK_NORM_W]
            for h in range(QK_NORM_W // LANE):
                sl = yn[:, h * LANE:(h + 1) * LANE]
                lo = c * QK_NORM_W + h * LANE
                p_ref[:, lo:lo + LANE] = (sl * cos + pltpu.roll(sl, LANE // 2, 1) * sin).astype(bf16)

    @pl.when(jnp.logical_and(j >= n_qk, j < n_plain_end))
    def _():
        p_ref[...] = y.astype(bf16)

    @pl.when(j >= n_plain_end)
    def _():
        p_ref[...] = jax.nn.sigmoid(y).astype(bf16)


def _inproj(xall, mod3, row_fn, rope_fn, mix_norm, w_p, w_dt, cos_tab, sin_tab, gmat, qkw, tm, tn):
    t, d = xall.shape
    n = w_p.shape[1]
    n_qk = 2 * ATT_WIDTH // tn
    kern = functools.partial(_inproj_kernel, n_qk=n_qk, n_plain_end=P_GA // tn)
    return pl.pallas_call(
        kern,
        grid=(t // tm, n // tn),
        in_specs=[
            pl.BlockSpec((tm, d), lambda i, j: (i, 0)),
            _mod_spec(d, row_fn, 3),
            _mod_spec(d, row_fn, 4),
            pl.BlockSpec((1, d), lambda i, j: (0, 0)),
            pl.BlockSpec((d, tn), lambda i, j: (0, j)),
            pl.BlockSpec((d, 2 * LANE), lambda i, j: (0, 0)),
            pl.BlockSpec((tm, LANE), lambda i, j: (rope_fn(i), 0)),
            pl.BlockSpec((tm, LANE), lambda i, j: (rope_fn(i), 0)),
            pl.BlockSpec((QK_NORM_W, QK_NORM_W), lambda i, j: (0, 0)),
            pl.BlockSpec((None, 1, tn), lambda i, j: (jnp.minimum(j // (n_qk // 2), 1), 0, 0)),
        ],
        out_specs=[
            pl.BlockSpec((tm, tn), lambda i, j: (i, j)),
            pl.BlockSpec((tm, 2 * LANE), lambda i, j: (i, 0)),
        ],
        out_shape=[jax.ShapeDtypeStruct((t, n), bf16), jax.ShapeDtypeStruct((t, 2 * LANE), f32)],
        scratch_shapes=[pltpu.VMEM((tm, d), bf16)],
        compiler_params=_cparams(("parallel", "arbitrary")),
    )(xall, mod3, mod3, mix_norm.reshape(1, d), w_p, w_dt, cos_tab, sin_tab, gmat, qkw)


def _attn_kernel(*refs, n_seg, lam_init):
    q_ref = refs[0]
    kv_refs = refs[1:1 + 2 * n_seg]
    lq1_ref, lk1_ref, lq2_ref, lk2_ref, sub_ref, o_ref = refs[1 + 2 * n_seg:]

    lam = (jnp.exp(jnp.sum(lq1_ref[...] * lk1_ref[...], axis=-1, keepdims=True))
           - jnp.exp(jnp.sum(lq2_ref[...] * lk2_ref[...], axis=-1, keepdims=True)) + lam_init)

    q = q_ref[...]
    comp = (lax.broadcasted_iota(jnp.int32, q.shape, 1) // (ATT_DH // 2)) % 2
    qc = [jnp.where(comp == 0, q, jnp.zeros_like(q)), jnp.where(comp == 1, q, jnp.zeros_like(q))]

    s = [[lax.dot_general(qc[c], kv_refs[2 * g][...], (((1,), (1,)), ((), ())), preferred_element_type=f32)
          for g in range(n_seg)] for c in range(2)]
    coef = []
    e = []
    for c in range(2):
        m = functools.reduce(jnp.maximum, [jnp.max(sg, axis=-1, keepdims=True) for sg in s[c]])
        ec = [jnp.exp2(sg - m) for sg in s[c]]
        den = functools.reduce(lambda u, v: u + v, [jnp.sum(eg, axis=-1, keepdims=True) for eg in ec])
        e.append(ec)
        coef.append(1.0 / den)
    coef[1] = coef[1] * lam
    o = None
    for g in range(n_seg):
        a = (e[0][g] * coef[0] - e[1][g] * coef[1]).astype(bf16)
        og = jnp.dot(a, kv_refs[2 * g + 1][...], preferred_element_type=f32)
        o = og if o is None else o + og
    on = o * lax.rsqrt(jnp.mean(o * o, axis=-1, keepdims=True) + EPS) * sub_ref[...] * (1.0 - lam_init)
    o_ref[...] = on.astype(bf16)


def _attention(p, n_q_rows, q_row0, lq, tq, segs, lam_init, lam_params, subln):
    nb = n_q_rows // lq
    nqt = lq // tq
    in_specs = [pl.BlockSpec((tq, LANE), lambda b, h, i: (q_row0 // tq + b * nqt + i, P_Q // LANE + h))]
    args = [p]
    for row0, ls in segs:
        in_specs.append(pl.BlockSpec((ls, LANE), functools.partial(
            lambda b, h, i, r0, col: (r0 + b, col + h), r0=row0 // ls, col=P_K // LANE)))
        in_specs.append(pl.BlockSpec((ls, LANE), functools.partial(
            lambda b, h, i, r0, col: (r0 + b, col + h), r0=row0 // ls, col=P_V // LANE)))
        args += [p, p]
    for lp in lam_params:
        in_specs.append(pl.BlockSpec((1, ATT_DH), lambda b, h, i: (0, 0)))
        args.append(lp.reshape(1, ATT_DH))
    in_specs.append(pl.BlockSpec((1, ATT_VD), lambda b, h, i: (0, 0)))
    args.append(subln.reshape(1, ATT_VD))
    kern = functools.partial(_attn_kernel, n_seg=len(segs), lam_init=lam_init)
    return pl.pallas_call(
        kern,
        grid=(nb, ATT_HEADS, nqt),
        in_specs=in_specs,
        out_specs=pl.BlockSpec((tq, LANE), lambda b, h, i: (b * nqt + i, h)),
        out_shape=jax.ShapeDtypeStruct((n_q_rows, ATT_WIDTH), bf16),
        compiler_params=_cparams(("parallel", "parallel", "arbitrary")),
    )(*args)


def _conv_kernel(x_ref, prev_ref, next_ref, w_ref, b_ref, o_ref, pad_ref, *, rows, n_lat_blocks, lat_per_seq,
                 ctx_per_seq):
    i = pl.program_id(0)
    is_lat = i < n_lat_blocks
    per = jnp.where(is_lat, lat_per_seq, ctx_per_seq)
    pos = jnp.where(is_lat, i % lat_per_seq, (i - n_lat_blocks) % ctx_per_seq)
    keep_prev = (pos != 0).astype(f32)
    keep_next = (pos != per - 1).astype(f32)
    pad_ref[0:HALO, :] = prev_ref[...].astype(f32) * keep_prev
    pad_ref[HALO:HALO + rows, :] = x_ref[...].astype(f32)
    pad_ref[HALO + rows:2 * HALO + rows, :] = next_ref[...].astype(f32) * keep_next
    acc = jnp.zeros((rows, x_ref.shape[1]), f32) + b_ref[...]
    for k in range(CONV_W):
        acc = acc + w_ref[k:k + 1, :] * pad_ref[pl.ds(HALO - CONV_W // 2 + k, rows), :]
    o_ref[...] = _silu(acc).astype(bf16)


def _conv(p, conv_w, conv_b, t_lat, l, lc):
    t = p.shape[0]
    rows = min(math.gcd(l, lc), 256)
    tc = CONV_DIM
    hb = rows // HALO
    c0 = P_XBC // tc
    kern = functools.partial(_conv_kernel, rows=rows, n_lat_blocks=t_lat // rows, lat_per_seq=l // rows,
                             ctx_per_seq=lc // rows)
    return pl.pallas_call(
        kern,
        grid=(t // rows, CONV_DIM // tc),
        in_specs=[
            pl.BlockSpec((rows, tc), lambda i, j: (i, c0 + j)),
            pl.BlockSpec((HALO, tc), lambda i, j: (jnp.maximum(i * hb - 1, 0), c0 + j)),
            pl.BlockSpec((HALO, tc), lambda i, j: (jnp.minimum((i + 1) * hb, t // HALO - 1), c0 + j)),
            pl.BlockSpec((CONV_W, tc), lambda i, j: (0, j)),
            pl.BlockSpec((1, tc), lambda i, j: (0, j)),
        ],
        out_specs=pl.BlockSpec((rows, tc), lambda i, j: (i, j)),
        out_shape=jax.ShapeDtypeStruct((t, CONV_DIM), bf16),
        scratch_shapes=[pltpu.VMEM((rows + 2 * HALO, tc), f32)],
        compiler_params=_cparams(("parallel", "parallel")),
    )(p, p, p, conv_w, conv_b.reshape(1, CONV_DIM))


def _ssd_kernel(xs_ref, b_ref, c_ref, dt_ref, bias_ref, alog_ref, y_ref, h_ref):
    d = pl.program_id(0)
    t = pl.program_id(2)
    ck = SSM_CHUNK
    hi = lax.Precision.HIGHEST

    @pl.when(t == 0)
    def _():
        h_ref[...] = jnp.zeros_like(h_ref)

    dtv = jax.nn.softplus(dt_ref[...] + bias_ref[...])
    adt = dtv * (-jnp.exp(alog_ref[...]))
    li = lax.broadcasted_iota(jnp.int32, (ck, ck), 0)
    si = lax.broadcasted_iota(jnp.int32, (ck, ck), 1)
    sgn = jnp.where(d == 0, 1, -1)
    mask = (si - li) * sgn <= 0
    mask_t = (li - si) * sgn <= 0
    acum = jnp.dot(mask.astype(f32), adt, precision=hi, preferred_element_type=f32)
    adt_t = adt.T
    dtv_t = dtv.T
    acum_t = jnp.dot(adt_t, mask_t.astype(f32), precision=hi, preferred_element_type=f32)
    tot_t = jnp.sum(adt_t, axis=1, keepdims=True)
    w_t = jnp.exp(tot_t - acum_t) * dtv_t
    e_col = jnp.exp(acum)
    dec_row = jnp.exp(jnp.sum(adt, axis=0, keepdims=True))

    lane_head = lax.broadcasted_iota(jnp.int32, (ck, GROUP_W), 1) // SSM_HEADDIM
    lane_head1 = lane_head[0:1, :]
    for g in range(SSM_GROUPS):
        bg = b_ref[:, g * SSM_STATE:(g + 1) * SSM_STATE]
        cg = c_ref[:, g * SSM_STATE:(g + 1) * SSM_STATE]
        cb = lax.dot_general(cg, bg, (((1,), (1,)), ((), ())), preferred_element_type=f32)
        bg_t = bg.astype(f32).T
        cg_f = cg.astype(f32)
        xg = xs_ref[:, g * GROUP_W:(g + 1) * GROUP_W]
        hg = h_ref[g]
        xm = jnp.concatenate([jnp.where(lane_head == r, xg, jnp.zeros_like(xg)) for r in range(SSM_HPG)], axis=0)
        hm = jnp.concatenate([jnp.where(lane_head == r, hg, 0.0).astype(bf16) for r in range(SSM_HPG)], axis=0)
        lhs_diag, lhs_off, lhs_st = [], [], []
        dec_g = jnp.zeros((1, GROUP_W), f32)
        for r in range(SSM_HPG):
            hh = g * SSM_HPG + r
            seg = jnp.broadcast_to(acum[:, hh:hh + 1], (ck, ck)) - acum_t[hh:hh + 1, :]
            decay = jnp.exp(jnp.where(mask, seg, NEG_BIG))
            lhs_diag.append((cb * decay * dtv_t[hh:hh + 1, :]).astype(bf16))
            lhs_off.append((cg_f * jnp.broadcast_to(e_col[:, hh:hh + 1], (ck, SSM_STATE))).astype(bf16))
            lhs_st.append((bg_t * w_t[hh:hh + 1, :]).astype(bf16))
            dec_g = jnp.where(lane_head1 == r, jnp.broadcast_to(dec_row[:, hh:hh + 1], (1, GROUP_W)), dec_g)
        y = jnp.dot(jnp.concatenate(lhs_diag + lhs_off, axis=1), jnp.concatenate([xm, hm], axis=0),
                    preferred_element_type=f32)
        y_ref[:, g * GROUP_W:(g + 1) * GROUP_W] = y.astype(bf16)
        st = jnp.dot(jnp.concatenate(lhs_st, axis=1), xm, preferred_element_type=f32)
        h_ref[g] = hg * dec_g + st


def _ssd(xbc, dt_raw, dt_bias, a_log, nb, t_lat, l, lc):
    t = xbc.shape[0]
    ck = SSM_CHUNK
    nck, nlk = lc // ck, l // ck

    def rowblk(d, b, s):
        tc = jnp.where(d == 0, s, nck - 1 - s)
        tl = jnp.where(d == 0, s - nck, nlk - 1 - (s - nck))
        return jnp.where(s < nck, t_lat // ck + b * nck + tc, b * nlk + tl)

    bc0 = SSM_INNER // (SSM_GROUPS * SSM_STATE)
    pad = lambda v: jnp.pad(v.astype(f32), ((0, 0), (0, LANE - SSM_HEADS))).reshape(2, 1, LANE)
    return pl.pallas_call(
        _ssd_kernel,
        grid=(2, nb, nck + nlk),
        in_specs=[
            pl.BlockSpec((ck, SSM_INNER), lambda d, b, s: (rowblk(d, b, s), 0)),
            pl.BlockSpec((ck, SSM_GROUPS * SSM_STATE), lambda d, b, s: (rowblk(d, b, s), bc0)),
            pl.BlockSpec((ck, SSM_GROUPS * SSM_STATE), lambda d, b, s: (rowblk(d, b, s), bc0 + 1)),
            pl.BlockSpec((ck, LANE), lambda d, b, s: (rowblk(d, b, s), d)),
            pl.BlockSpec((None, 1, LANE), lambda d, b, s: (d, 0, 0)),
            pl.BlockSpec((None, 1, LANE), lambda d, b, s: (d, 0, 0)),
        ],
        out_specs=pl.BlockSpec((None, ck, SSM_INNER), lambda d, b, s: (d, rowblk(d, b, s), 0)),
        out_shape=jax.ShapeDtypeStruct((2, t, SSM_INNER), bf16),
        scratch_shapes=[pltpu.VMEM((SSM_GROUPS, SSM_STATE, GROUP_W), f32)],
        compiler_params=_cparams(("parallel", "parallel", "arbitrary")),
    )(xbc, xbc, xbc, dt_raw, pad(dt_bias), pad(a_log))


def _post_kernel(oal_ref, oac_ref, yf_ref, yb_ref, xs_ref, z_ref, ga_ref, gs_ref, x_ref, gt_ref, dsk_ref, nw_ref,
                 wa_ref, ws_ref, wo_ref, o_ref, *, n_lat_tiles):
    i = pl.program_id(0)
    y = yf_ref[...].astype(f32) + yb_ref[...].astype(f32) + dsk_ref[...] * xs_ref[...].astype(f32)
    gz = y * _silu(z_ref[...].astype(f32))
    gw = SSM_INNER // SSM_GROUPS
    parts = []
    for g in range(SSM_GROUPS):
        v = gz[:, g * gw:(g + 1) * gw]
        vn = v * lax.rsqrt(jnp.mean(v * v, axis=-1, keepdims=True) + EPS) * nw_ref[:, g * gw:(g + 1) * gw]
        parts.append(vn.astype(bf16))
    o_ssm = jnp.concatenate(parts, axis=1)
    o_attn = jnp.where(i < n_lat_tiles, oal_ref[...], oac_ref[...])
    ua = jnp.dot(o_attn, wa_ref[...], preferred_element_type=f32)
    us = jnp.dot(o_ssm, ws_ref[...], preferred_element_type=f32)
    u = (ga_ref[...].astype(f32) * ua + gs_ref[...].astype(f32) * us).astype(bf16)
    o_ref[...] = x_ref[...] + gt_ref[...] * jnp.dot(u, wo_ref[...], preferred_element_type=f32)


def _post(o_attn_lat, o_attn_ctx, n_lat_tiles, y2, xbc, p, xall, n_rows, mod3, row_fn, d_skip_lanes, ssm_norm,
          w_ba, w_bs, w_o, tp):
    d = w_o.shape[1]
    si = SSM_INNER
    kern = functools.partial(_post_kernel, n_lat_tiles=n_lat_tiles)
    return pl.pallas_call(
        kern,
        grid=(n_rows // tp,),
        in_specs=[
            pl.BlockSpec((tp, ATT_WIDTH), lambda i: (jnp.minimum(i, n_lat_tiles - 1), 0)),
            pl.BlockSpec((tp, ATT_WIDTH), lambda i: (jnp.maximum(i - n_lat_tiles, 0), 0)),
            pl.BlockSpec((None, tp, si), lambda i: (0, i, 0)),
            pl.BlockSpec((None, tp, si), lambda i: (1, i, 0)),
            pl.BlockSpec((tp, si), lambda i: (i, 0)),
            pl.BlockSpec((tp, si), lambda i: (i, P_Z // si)),
            pl.BlockSpec((tp, d), lambda i: (i, P_GA // d)),
            pl.BlockSpec((tp, d), lambda i: (i, P_GA // d + 1)),
            pl.BlockSpec((tp, d), lambda i: (i, 0)),
            _mod_spec(d, row_fn, 5),
            pl.BlockSpec((1, si), lambda i: (0, 0)),
            pl.BlockSpec((1, si), lambda i: (0, 0)),
            _resident((ATT_WIDTH, d)),
            _resident((si, d)),
            _resident((d, d)),
        ],
        out_specs=pl.BlockSpec((tp, d), lambda i: (i, 0)),
        out_shape=jax.ShapeDtypeStruct((n_rows, d), f32),
        compiler_params=_cparams(("parallel",)),
    )(o_attn_lat, o_attn_ctx, y2, y2, xbc, p, p, p, xall, mod3, d_skip_lanes, ssm_norm.reshape(1, si),
      w_ba, w_bs, w_o)


def _qk_layout():
    lane = jnp.arange(LANE)
    par, comp, freq = lane // 64, (lane // 32) % 2, lane % 32
    within = comp * ATT_DH + 2 * freq + par
    return (jnp.arange(ATT_HEADS)[:, None] * LANE + within[None, :]).reshape(-1), (2 * freq + par)


def _rope_tables(l, tm):
    n_freq = ATT_DH // 4
    rows = l // GRID_W
    inv = ROPE_BASE ** (-jnp.arange(n_freq, dtype=f32) / n_freq)
    row = jnp.repeat(jnp.arange(rows, dtype=f32), GRID_W)
    col = jnp.tile(jnp.arange(GRID_W, dtype=f32), rows)
    ang = jnp.concatenate([row[:, None] * inv, col[:, None] * inv], axis=-1)
    cos, sin = jnp.cos(ang), jnp.sin(ang)
    cos_l = jnp.tile(cos, (1, LANE // (ATT_DH // 2)))
    sin_l = jnp.concatenate([-sin, -sin, sin, sin], axis=-1)
    cos_tab = jnp.concatenate([cos_l, jnp.ones((tm, LANE), f32)], axis=0)
    sin_tab = jnp.concatenate([sin_l, jnp.zeros((tm, LANE), f32)], axis=0)
    return cos_tab, sin_tab


def _tiles(l, lc, nb):
    tm = min(1024, l)
    tfm = min(512, l)
    tp = min(256, l)
    tq = min(256, l)
    for tile in (tm, tfm, tp):
        assert l % tile == 0 and (nb * lc) % tile == 0
    return tm, tfm, tp, tq


def kernel(x, c, ctx, c_ctx, ada_w, ada_b, ffn1_norm, ffn1_w_gu, ffn1_w_down, mix_norm, w_in, q_norm, k_norm,
           lambda_q1, lambda_k1, lambda_q2, lambda_k2, attn_subln, conv_w, conv_b, dt_bias, a_log, d_skip,
           ssm_norm, w_branch_attn, w_branch_ssm, w_out, ffn2_norm, ffn2_w_gu, ffn2_w_down):
    nb, l, d = x.shape
    lc = ctx.shape[1]
    depth = ada_w.shape[0]
    t_lat, t_ctx = nb * l, nb * lc
    t = t_lat + t_ctx
    tm, tfm, tp, tq = _tiles(l, lc, nb)
    tf = 512
    tn = 1024
    assert nb < COND_ROWS and l % GRID_W == 0

    xall = jnp.concatenate([x.reshape(t_lat, d), ctx.reshape(t_ctx, d)], axis=0)
    conds = jnp.zeros((COND_ROWS, d), f32).at[:nb].set(c).at[nb].set(c_ctx)
    mod = _adaln(conds, ada_w, ada_b)
    mod3 = mod.reshape(depth * COND_ROWS, 1, N_MOD * d)

    perm, norm_idx = _qk_layout()
    cos_tab, sin_tab = _rope_tables(l, tm)
    rope_fn = lambda i: jnp.where(i < t_lat // tm, i % (l // tm), l // tm)
    lane_n = jnp.arange(QK_NORM_W)
    grp = (lane_n // LANE) * 2 + (lane_n // 32) % 2
    gmat = (grp[:, None] == grp[None, :]).astype(bf16)

    s0 = 3 * ATT_WIDTH
    dt0 = s0 + SSM_INNER + CONV_DIM
    g0 = dt0 + 2 * SSM_HEADS

    def cond_row(tile_rows, base):
        return lambda ti: base + jnp.where(ti < t_lat // tile_rows, ti // (l // tile_rows), nb)

    for i in range(depth):
        last = i == depth - 1
        row_m = cond_row(tm, i * COND_ROWS)
        row_f = cond_row(tfm, i * COND_ROWS)
        row_p = cond_row(tp, i * COND_ROWS)
        lam_init = 0.8 - 0.6 * math.exp(-0.3 * i)

        w = w_in[i]
        w_p = jnp.concatenate([
            w[:, perm], w[:, ATT_WIDTH + perm], w[:, 2 * ATT_WIDTH:s0],
            w[:, s0 + SSM_INNER:dt0], w[:, s0:s0 + SSM_INNER], w[:, g0:]], axis=1).astype(bf16)
        zpad = jnp.zeros((d, LANE - SSM_HEADS), f32)
        w_dt = jnp.concatenate([w[:, dt0:dt0 + SSM_HEADS], zpad, w[:, dt0 + SSM_HEADS:g0], zpad],
                               axis=1).astype(bf16)
        qkw = jnp.stack([jnp.tile(q_norm[i][norm_idx] * (LOG2E * ATT_DH ** -0.5), ATT_HEADS),
                         jnp.tile(k_norm[i][norm_idx], ATT_HEADS)]).reshape(2, 1, ATT_WIDTH)
        w_gu1, w_d1 = ffn1_w_gu[i].astype(bf16), ffn1_w_down[i].astype(bf16)
        w_gu2, w_d2 = ffn2_w_gu[i].astype(bf16), ffn2_w_down[i].astype(bf16)
        w_ba, w_bs, w_o = w_branch_attn[i].astype(bf16), w_branch_ssm[i].astype(bf16), w_out[i].astype(bf16)
        lam_params = (lambda_q1[i], lambda_k1[i], lambda_q2[i], lambda_k2[i])
        d_skip_lanes = jnp.repeat(d_skip[i], SSM_HEADDIM).reshape(1, SSM_INNER)

        xall = _ffn(xall, t, mod3, row_f, 0, ffn1_norm[i], w_gu1, w_d1, tfm, tf)

        p, dt_raw = _inproj(xall, mod3, row_m, rope_fn, mix_norm[i], w_p, w_dt, cos_tab, sin_tab, gmat, qkw,
                            tm, tn)
        o_attn = _attention(p, t_lat, 0, l, tq, [(0, l), (t_lat, lc)], lam_init, lam_params, attn_subln[i])
        xbc = _conv(p, conv_w[i], conv_b[i], t_lat, l, lc)
        y2 = _ssd(xbc, dt_raw, dt_bias[i], a_log[i], nb, t_lat, l, lc)
        if last:
            n_rows, o_attn_c = t_lat, o_attn
        else:
            n_rows = t
            o_attn_c = _attention(p, t_ctx, t_lat, lc, min(tq, lc), [(t_lat, lc)], lam_init, lam_params,
                                  attn_subln[i])
        xall = _post(o_attn, o_attn_c, t_lat // tp, y2, xbc, p, xall, n_rows, mod3, row_p, d_skip_lanes,
                     ssm_norm[i], w_ba, w_bs, w_o, tp)

        xall = _ffn(xall, n_rows, mod3, row_f, 2, ffn2_norm[i], w_gu2, w_d2, tfm, tf)

    return xall[:t_lat].reshape(nb, l, d)
```

```python
import functools
import math

import jax
import jax.numpy as jnp
from jax import lax
from jax.experimental import pallas as pl
from jax.experimental.pallas import tpu as pltpu

f32 = jnp.float32
bf16 = jnp.bfloat16

EPS = 1e-6
N_MOD = 9
FFN_RESIDUAL = 0.5
GRID_W = 64
ROPE_BASE = 10000.0

ATT_HEADS = 8
ATT_DH = 64
ATT_VD = 2 * ATT_DH
ATT_WIDTH = ATT_HEADS * ATT_VD

SSM_INNER = 2048
SSM_HEADDIM = 64
SSM_HEADS = SSM_INNER // SSM_HEADDIM
SSM_GROUPS = 4
SSM_HPG = SSM_HEADS // SSM_GROUPS
SSM_STATE = 128
SSM_CHUNK = 128
CONV_W = 5
CONV_DIM = SSM_INNER + 2 * SSM_GROUPS * SSM_STATE
GROUP_W = SSM_HPG * SSM_HEADDIM

LANE = 128
COND_ROWS = 16
HALO = 16
NEG_BIG = -1e30
LOG2E = math.log2(math.e)
QK_NORM_W = 512

P_Q, P_K, P_V = 0, ATT_WIDTH, 2 * ATT_WIDTH
P_XBC = 3 * ATT_WIDTH
P_Z = P_XBC + CONV_DIM
P_GA = P_Z + SSM_INNER

VMEM_LIMIT = 58 * 1024 * 1024


def _cparams(sem):
    return pltpu.CompilerParams(dimension_semantics=sem, vmem_limit_bytes=VMEM_LIMIT)


def _resident(shape):
    return pl.BlockSpec(shape, lambda *_: (0,) * len(shape), pipeline_mode=pl.Buffered(1))


def _silu(v):
    return v * jax.nn.sigmoid(v)


def _mod_norm(x, nw, shift, scale):
    y = x * lax.rsqrt(jnp.mean(x * x, axis=-1, keepdims=True) + EPS)
    return (y * nw) * (1.0 + scale) + shift


def _adaln_kernel(c_ref, w_ref, b_ref, o_ref):
    c = c_ref[...]
    s = _silu(c).astype(bf16)
    o_ref[...] = jnp.dot(s, w_ref[...].astype(bf16), preferred_element_type=f32) + b_ref[...]


def _adaln(conds, ada_w, ada_b):
    depth, d, n = ada_w.shape
    tn = 1024
    return pl.pallas_call(
        _adaln_kernel,
        grid=(depth, n // tn),
        in_specs=[
            pl.BlockSpec((COND_ROWS, d), lambda l, j: (0, 0)),
            pl.BlockSpec((None, d, tn), lambda l, j: (l, 0, j)),
            pl.BlockSpec((None, 1, tn), lambda l, j: (l, 0, j)),
        ],
        out_specs=pl.BlockSpec((None, COND_ROWS, tn), lambda l, j: (l, 0, j)),
        out_shape=jax.ShapeDtypeStruct((depth, COND_ROWS, n), f32),
        compiler_params=_cparams(("parallel", "parallel")),
    )(conds, ada_w, ada_b.reshape(depth, 1, n))


def _ffn_kernel(x_ref, sh_ref, sc_ref, gt_ref, nw_ref, wg_ref, wv_ref, wd_ref, o_ref, xn_ref, acc_ref):
    j = pl.program_id(1)

    @pl.when(j == 0)
    def _():
        xn_ref[...] = _mod_norm(x_ref[...], nw_ref[...], sh_ref[...], sc_ref[...]).astype(bf16)
        acc_ref[...] = jnp.zeros_like(acc_ref)

    xn = xn_ref[...]
    g = jnp.dot(xn, wg_ref[...], preferred_element_type=f32)
    v = jnp.dot(xn, wv_ref[...], preferred_element_type=f32)
    a = (_silu(g) * v).astype(bf16)
    acc_ref[...] += jnp.dot(a, wd_ref[...], preferred_element_type=f32)

    @pl.when(j == pl.num_programs(1) - 1)
    def _():
        o_ref[...] = x_ref[...] + (FFN_RESIDUAL * gt_ref[...]) * acc_ref[...]


def _mod_spec(d, row_fn, col):
    return pl.BlockSpec((None, 1, d), lambda i, *_: (row_fn(i), 0, col))


def _ffn(xall, n_rows, mod3, row_fn, slot, norm_w, w_gu, w_down, tm, tf):
    d = xall.shape[1]
    ff = w_down.shape[0]
    nf = ff // tf
    return pl.pallas_call(
        _ffn_kernel,
        grid=(n_rows // tm, nf),
        in_specs=[
            pl.BlockSpec((tm, d), lambda i, j: (i, 0)),
            _mod_spec(d, row_fn, 3 * slot),
            _mod_spec(d, row_fn, 3 * slot + 1),
            _mod_spec(d, row_fn, 3 * slot + 2),
            pl.BlockSpec((1, d), lambda i, j: (0, 0)),
            pl.BlockSpec((d, tf), lambda i, j: (0, j)),
            pl.BlockSpec((d, tf), lambda i, j: (0, nf + j)),
            pl.BlockSpec((tf, d), lambda i, j: (j, 0)),
        ],
        out_specs=pl.BlockSpec((tm, d), lambda i, j: (i, 0)),
        out_shape=jax.ShapeDtypeStruct((n_rows, d), f32),
        scratch_shapes=[pltpu.VMEM((tm, d), bf16), pltpu.VMEM((tm, d), f32)],
        compiler_params=_cparams(("parallel", "arbitrary")),
    )(xall, mod3, mod3, mod3, norm_w.reshape(1, d), w_gu, w_gu, w_down)


def _inproj_kernel(x_ref, sh_ref, sc_ref, nw_ref, w_ref, wdt_ref, cos_ref, sin_ref, g_ref, qkw_ref,
                   p_ref, dt_ref, xn_ref, *, n_qk, n_plain_end):
    j = pl.program_id(1)

    @pl.when(j == 0)
    def _():
        xn = _mod_norm(x_ref[...], nw_ref[...], sh_ref[...], sc_ref[...]).astype(bf16)
        xn_ref[...] = xn
        dt_ref[...] = jnp.dot(xn, wdt_ref[...], preferred_element_type=f32)

    y = jnp.dot(xn_ref[...], w_ref[...], preferred_element_type=f32)

    @pl.when(j < n_qk)
    def _():
        cos, sin = cos_ref[...], sin_ref[...]
        for c in range(y.shape[1] // QK_NORM_W):
            yc = y[:, c * QK_NORM_W:(c + 1) * QK_NORM_W]
            ss = jnp.dot((yc * yc).astype(bf16), g_ref[...], preferred_element_type=f32)
            yn = yc * lax.rsqrt(ss * (1.0 / ATT_DH) + EPS) * qkw_ref[:, c * QK_NORM_W:(c + 1) * QK_NORM_W]
            for h in range(QK_NORM_W // LANE):
                sl = yn[:, h * LANE:(h + 1) * LANE]
                lo = c * QK_NORM_W + h * LANE
                p_ref[:, lo:lo + LANE] = (sl * cos + pltpu.roll(sl, LANE // 2, 1) * sin).astype(bf16)

    @pl.when(jnp.logical_and(j >= n_qk, j < n_plain_end))
    def _():
        p_ref[...] = y.astype(bf16)

    @pl.when(j >= n_plain_end)
    def _():
        p_ref[...] = jax.nn.sigmoid(y).astype(bf16)


def _inproj(xall, mod3, row_fn, rope_fn, mix_norm, w_p, w_dt, cos_tab, sin_tab, gmat, qkw, tm, tn):
    t, d = xall.shape
    n = w_p.shape[1]
    n_qk = 2 * ATT_WIDTH // tn
    kern = functools.partial(_inproj_kernel, n_qk=n_qk, n_plain_end=P_GA // tn)
    return pl.pallas_call(
        kern,
        grid=(t // tm, n // tn),
        in_specs=[
            pl.BlockSpec((tm, d), lambda i, j: (i, 0)),
            _mod_spec(d, row_fn, 3),
            _mod_spec(d, row_fn, 4),
            pl.BlockSpec((1, d), lambda i, j: (0, 0)),
            pl.BlockSpec((d, tn), lambda i, j: (0, j)),
            pl.BlockSpec((d, 2 * LANE), lambda i, j: (0, 0)),
            pl.BlockSpec((tm, LANE), lambda i, j: (rope_fn(i), 0)),
            pl.BlockSpec((tm, LANE), lambda i, j: (rope_fn(i), 0)),
            pl.BlockSpec((QK_NORM_W, QK_NORM_W), lambda i, j: (0, 0)),
            pl.BlockSpec((None, 1, tn), lambda i, j: (jnp.minimum(j // (n_qk // 2), 1), 0, 0)),
        ],
        out_specs=[
            pl.BlockSpec((tm, tn), lambda i, j: (i, j)),
            pl.BlockSpec((tm, 2 * LANE), lambda i, j: (i, 0)),
        ],
        out_shape=[jax.ShapeDtypeStruct((t, n), bf16), jax.ShapeDtypeStruct((t, 2 * LANE), f32)],
        scratch_shapes=[pltpu.VMEM((tm, d), bf16)],
        compiler_params=_cparams(("parallel", "arbitrary")),
    )(xall, mod3, mod3, mix_norm.reshape(1, d), w_p, w_dt, cos_tab, sin_tab, gmat, qkw)


def _attn_kernel(*refs, n_seg, lam_init):
    q_ref = refs[0]
    kv_refs = refs[1:1 + 2 * n_seg]
    lq1_ref, lk1_ref, lq2_ref, lk2_ref, sub_ref, o_ref = refs[1 + 2 * n_seg:]

    lam = (jnp.exp(jnp.sum(lq1_ref[...] * lk1_ref[...], axis=-1, keepdims=True))
           - jnp.exp(jnp.sum(lq2_ref[...] * lk2_ref[...], axis=-1, keepdims=True)) + lam_init)

    q = q_ref[...]
    comp = (lax.broadcasted_iota(jnp.int32, q.shape, 1) // (ATT_DH // 2)) % 2
    qc = [jnp.where(comp == 0, q, jnp.zeros_like(q)), jnp.where(comp == 1, q, jnp.zeros_like(q))]

    s = [[lax.dot_general(qc[c], kv_refs[2 * g][...], (((1,), (1,)), ((), ())), preferred_element_type=f32)
          for g in range(n_seg)] for c in range(2)]
    coef = []
    e = []
    for c in range(2):
        m = functools.reduce(jnp.maximum, [jnp.max(sg, axis=-1, keepdims=True) for sg in s[c]])
        ec = [jnp.exp2(sg - m) for sg in s[c]]
        den = functools.reduce(lambda u, v: u + v, [jnp.sum(eg, axis=-1, keepdims=True) for eg in ec])
        e.append(ec)
        coef.append(1.0 / den)
    coef[1] = coef[1] * lam
    o = None
    for g in range(n_seg):
        a = (e[0][g] * coef[0] - e[1][g] * coef[1]).astype(bf16)
        og = jnp.dot(a, kv_refs[2 * g + 1][...], preferred_element_type=f32)
        o = og if o is None else o + og
    on = o * lax.rsqrt(jnp.mean(o * o, axis=-1, keepdims=True) + EPS) * sub_ref[...] * (1.0 - lam_init)
    o_ref[...] = on.astype(bf16)


def _attention(p, n_q_rows, q_row0, lq, tq, segs, lam_init, lam_params, subln):
    nb = n_q_rows // lq
    nqt = lq // tq
    in_specs = [pl.BlockSpec((tq, LANE), lambda b, h, i: (q_row0 // tq + b * nqt + i, P_Q // LANE + h))]
    args = [p]
    for row0, ls in segs:
        in_specs.append(pl.BlockSpec((ls, LANE), functools.partial(
            lambda b, h, i, r0, col: (r0 + b, col + h), r0=row0 // ls, col=P_K // LANE)))
        in_specs.append(pl.BlockSpec((ls, LANE), functools.partial(
            lambda b, h, i, r0, col: (r0 + b, col + h), r0=row0 // ls, col=P_V // LANE)))
        args += [p, p]
    for lp in lam_params:
        in_specs.append(pl.BlockSpec((1, ATT_DH), lambda b, h, i: (0, 0)))
        args.append(lp.reshape(1, ATT_DH))
    in_specs.append(pl.BlockSpec((1, ATT_VD), lambda b, h, i: (0, 0)))
    args.append(subln.reshape(1, ATT_VD))
    kern = functools.partial(_attn_kernel, n_seg=len(segs), lam_init=lam_init)
    return pl.pallas_call(
        kern,
        grid=(nb, ATT_HEADS, nqt),
        in_specs=in_specs,
        out_specs=pl.BlockSpec((tq, LANE), lambda b, h, i: (b * nqt + i, h)),
        out_shape=jax.ShapeDtypeStruct((n_q_rows, ATT_WIDTH), bf16),
        compiler_params=_cparams(("parallel", "parallel", "arbitrary")),
    )(*args)


def _conv_kernel(x_ref, prev_ref, next_ref, w_ref, b_ref, o_ref, pad_ref, *, rows, n_lat_blocks, lat_per_seq,
                 ctx_per_seq):
    i = pl.program_id(0)
    is_lat = i < n_lat_blocks
    per = jnp.where(is_lat, lat_per_seq, ctx_per_seq)
    pos = jnp.where(is_lat, i % lat_per_seq, (i - n_lat_blocks) % ctx_per_seq)
    keep_prev = (pos != 0).astype(f32)
    keep_next = (pos != per - 1).astype(f32)
    pad_ref[0:HALO, :] = prev_ref[...].astype(f32) * keep_prev
    pad_ref[HALO:HALO + rows, :] = x_ref[...].astype(f32)
    pad_ref[HALO + rows:2 * HALO + rows, :] = next_ref[...].astype(f32) * keep_next
    acc = jnp.zeros((rows, x_ref.shape[1]), f32) + b_ref[...]
    for k in range(CONV_W):
        acc = acc + w_ref[k:k + 1, :] * pad_ref[pl.ds(HALO - CONV_W // 2 + k, rows), :]
    o_ref[...] = _silu(acc).astype(bf16)


def _conv(p, conv_w, conv_b, t_lat, l, lc):
    t = p.shape[0]
    rows = min(math.gcd(l, lc), 256)
    tc = CONV_DIM
    hb = rows // HALO
    c0 = P_XBC // tc
    kern = functools.partial(_conv_kernel, rows=rows, n_lat_blocks=t_lat // rows, lat_per_seq=l // rows,
                             ctx_per_seq=lc // rows)
    return pl.pallas_call(
        kern,
        grid=(t // rows, CONV_DIM // tc),
        in_specs=[
            pl.BlockSpec((rows, tc), lambda i, j: (i, c0 + j)),
            pl.BlockSpec((HALO, tc), lambda i, j: (jnp.maximum(i * hb - 1, 0), c0 + j)),
            pl.BlockSpec((HALO, tc), lambda i, j: (jnp.minimum((i + 1) * hb, t // HALO - 1), c0 + j)),
            pl.BlockSpec((CONV_W, tc), lambda i, j: (0, j)),
            pl.BlockSpec((1, tc), lambda i, j: (0, j)),
        ],
        out_specs=pl.BlockSpec((rows, tc), lambda i, j: (i, j)),
        out_shape=jax.ShapeDtypeStruct((t, CONV_DIM), bf16),
        scratch_shapes=[pltpu.VMEM((rows + 2 * HALO, tc), f32)],
        compiler_params=_cparams(("parallel", "parallel")),
    )(p, p, p, conv_w, conv_b.reshape(1, CONV_DIM))


def _ssd_kernel(xs0, b0, c0, dt0, xs1, b1, c1, dt1, bias_ref, alog_ref, y0, y1, h0, h1):
    _ssd_dir(0, xs0, b0, c0, dt0, bias_ref, alog_ref, y0, h0)
    _ssd_dir(1, xs1, b1, c1, dt1, bias_ref, alog_ref, y1, h1)


def _ssd_dir(d, xs_ref, b_ref, c_ref, dt_ref, bias_ref, alog_ref, y_ref, h_ref):
    t = pl.program_id(1)
    ck = SSM_CHUNK
    hi = lax.Precision.HIGHEST

    @pl.when(t == 0)
    def _():
        h_ref[...] = jnp.zeros_like(h_ref)

    dtv = jax.nn.softplus(dt_ref[...] + bias_ref[d])
    adt = dtv * (-jnp.exp(alog_ref[d]))
    li = lax.broadcasted_iota(jnp.int32, (ck, ck), 0)
    si = lax.broadcasted_iota(jnp.int32, (ck, ck), 1)
    sgn = 1 if d == 0 else -1
    mask = (si - li) * sgn <= 0
    mask_t = (li - si) * sgn <= 0
    acum = jnp.dot(mask.astype(f32), adt, precision=hi, preferred_element_type=f32)
    adt_t = adt.T
    dtv_t = dtv.T
    acum_t = jnp.dot(adt_t, mask_t.astype(f32), precision=hi, preferred_element_type=f32)
    tot_t = jnp.sum(adt_t, axis=1, keepdims=True)
    w_t = jnp.exp(tot_t - acum_t) * dtv_t
    e_col = jnp.exp(acum)
    dec_row = jnp.exp(jnp.sum(adt, axis=0, keepdims=True))

    lane_head = lax.broadcasted_iota(jnp.int32, (ck, GROUP_W), 1) // SSM_HEADDIM
    lane_head1 = lane_head[0:1, :]
    for g in range(SSM_GROUPS):
        bg = b_ref[:, g * SSM_STATE:(g + 1) * SSM_STATE]
        cg = c_ref[:, g * SSM_STATE:(g + 1) * SSM_STATE]
        cb = lax.dot_general(cg, bg, (((1,), (1,)), ((), ())), preferred_element_type=f32)
        bg_t = bg.astype(f32).T
        cg_f = cg.astype(f32)
        xg = xs_ref[:, g * GROUP_W:(g + 1) * GROUP_W]
        hg = h_ref[g]
        xm = jnp.concatenate([jnp.where(lane_head == r, xg, jnp.zeros_like(xg)) for r in range(SSM_HPG)], axis=0)
        hm = jnp.concatenate([jnp.where(lane_head == r, hg, 0.0).astype(bf16) for r in range(SSM_HPG)], axis=0)
        lhs_diag, lhs_off, lhs_st = [], [], []
        dec_g = jnp.zeros((1, GROUP_W), f32)
        for r in range(SSM_HPG):
            hh = g * SSM_HPG + r
            seg = jnp.broadcast_to(acum[:, hh:hh + 1], (ck, ck)) - acum_t[hh:hh + 1, :]
            decay = jnp.exp(jnp.where(mask, seg, NEG_BIG))
            lhs_diag.append((cb * decay * dtv_t[hh:hh + 1, :]).astype(bf16))
            lhs_off.append((cg_f * jnp.broadcast_to(e_col[:, hh:hh + 1], (ck, SSM_STATE))).astype(bf16))
            lhs_st.append((bg_t * w_t[hh:hh + 1, :]).astype(bf16))
            dec_g = jnp.where(lane_head1 == r, jnp.broadcast_to(dec_row[:, hh:hh + 1], (1, GROUP_W)), dec_g)
        y = jnp.dot(jnp.concatenate(lhs_diag + lhs_off, axis=1), jnp.concatenate([xm, hm], axis=0),
                    preferred_element_type=f32)
        y_ref[:, g * GROUP_W:(g + 1) * GROUP_W] = y.astype(bf16)
        st = jnp.dot(jnp.concatenate(lhs_st, axis=1), xm, preferred_element_type=f32)
        h_ref[g] = hg * dec_g + st


def _ssd(xbc, dt_raw, dt_bias, a_log, nb, t_lat, l, lc):
    t = xbc.shape[0]
    ck = SSM_CHUNK
    nck, nlk = lc // ck, l // ck

    def rowblk(d, b, s):
        tc = jnp.where(d == 0, s, nck - 1 - s)
        tl = jnp.where(d == 0, s - nck, nlk - 1 - (s - nck))
        return jnp.where(s < nck, t_lat // ck + b * nck + tc, b * nlk + tl)

    bc0 = SSM_INNER // (SSM_GROUPS * SSM_STATE)
    pad = lambda v: jnp.pad(v.astype(f32), ((0, 0), (0, LANE - SSM_HEADS))).reshape(2, 1, LANE)

    def dir_specs(d):
        return [
            pl.BlockSpec((ck, SSM_INNER), lambda b, s: (rowblk(d, b, s), 0)),
            pl.BlockSpec((ck, SSM_GROUPS * SSM_STATE), lambda b, s: (rowblk(d, b, s), bc0)),
            pl.BlockSpec((ck, SSM_GROUPS * SSM_STATE), lambda b, s: (rowblk(d, b, s), bc0 + 1)),
            pl.BlockSpec((ck, LANE), lambda b, s: (rowblk(d, b, s), d)),
        ]

    state = pltpu.VMEM((SSM_GROUPS, SSM_STATE, GROUP_W), f32)
    y_f, y_b = pl.pallas_call(
        _ssd_kernel,
        grid=(nb, nck + nlk),
        in_specs=dir_specs(0) + dir_specs(1) + [
            pl.BlockSpec((2, 1, LANE), lambda b, s: (0, 0, 0)),
            pl.BlockSpec((2, 1, LANE), lambda b, s: (0, 0, 0)),
        ],
        out_specs=[pl.BlockSpec((ck, SSM_INNER), lambda b, s: (rowblk(0, b, s), 0)),
                   pl.BlockSpec((ck, SSM_INNER), lambda b, s: (rowblk(1, b, s), 0))],
        out_shape=[jax.ShapeDtypeStruct((t, SSM_INNER), bf16)] * 2,
        scratch_shapes=[state, state],
        compiler_params=_cparams(("parallel", "arbitrary")),
    )(xbc, xbc, xbc, dt_raw, xbc, xbc, xbc, dt_raw, pad(dt_bias), pad(a_log))
    return y_f, y_b


def _post_kernel(oal_ref, oac_ref, yf_ref, yb_ref, xs_ref, z_ref, ga_ref, gs_ref, x_ref, gt_ref, dsk_ref, nw_ref,
                 wa_ref, ws_ref, wo_ref, o_ref, *, n_lat_tiles):
    i = pl.program_id(0)
    y = yf_ref[...].astype(f32) + yb_ref[...].astype(f32) + dsk_ref[...] * xs_ref[...].astype(f32)
    gz = y * _silu(z_ref[...].astype(f32))
    gw = SSM_INNER // SSM_GROUPS
    parts = []
    for g in range(SSM_GROUPS):
        v = gz[:, g * gw:(g + 1) * gw]
        vn = v * lax.rsqrt(jnp.mean(v * v, axis=-1, keepdims=True) + EPS) * nw_ref[:, g * gw:(g + 1) * gw]
        parts.append(vn.astype(bf16))
    o_ssm = jnp.concatenate(parts, axis=1)
    o_attn = jnp.where(i < n_lat_tiles, oal_ref[...], oac_ref[...])
    ua = jnp.dot(o_attn, wa_ref[...], preferred_element_type=f32)
    us = jnp.dot(o_ssm, ws_ref[...], preferred_element_type=f32)
    u = (ga_ref[...].astype(f32) * ua + gs_ref[...].astype(f32) * us).astype(bf16)
    o_ref[...] = x_ref[...] + gt_ref[...] * jnp.dot(u, wo_ref[...], preferred_element_type=f32)


def _post(o_attn_lat, o_attn_ctx, n_lat_tiles, y2, xbc, p, xall, n_rows, mod3, row_fn, d_skip_lanes, ssm_norm,
          w_ba, w_bs, w_o, tp):
    d = w_o.shape[1]
    si = SSM_INNER
    kern = functools.partial(_post_kernel, n_lat_tiles=n_lat_tiles)
    return pl.pallas_call(
        kern,
        grid=(n_rows // tp,),
        in_specs=[
            pl.BlockSpec((tp, ATT_WIDTH), lambda i: (jnp.minimum(i, n_lat_tiles - 1), 0)),
            pl.BlockSpec((tp, ATT_WIDTH), lambda i: (jnp.maximum(i - n_lat_tiles, 0), 0)),
            pl.BlockSpec((tp, si), lambda i: (i, 0)),
            pl.BlockSpec((tp, si), lambda i: (i, 0)),
            pl.BlockSpec((tp, si), lambda i: (i, 0)),
            pl.BlockSpec((tp, si), lambda i: (i, P_Z // si)),
            pl.BlockSpec((tp, d), lambda i: (i, P_GA // d)),
            pl.BlockSpec((tp, d), lambda i: (i, P_GA // d + 1)),
            pl.BlockSpec((tp, d), lambda i: (i, 0)),
            _mod_spec(d, row_fn, 5),
            pl.BlockSpec((1, si), lambda i: (0, 0)),
            pl.BlockSpec((1, si), lambda i: (0, 0)),
            _resident((ATT_WIDTH, d)),
            _resident((si, d)),
            _resident((d, d)),
        ],
        out_specs=pl.BlockSpec((tp, d), lambda i: (i, 0)),
        out_shape=jax.ShapeDtypeStruct((n_rows, d), f32),
        compiler_params=_cparams(("parallel",)),
    )(o_attn_lat, o_attn_ctx, y2[0], y2[1], xbc, p, p, p, xall, mod3, d_skip_lanes, ssm_norm.reshape(1, si),
      w_ba, w_bs, w_o)


def _qk_layout():
    lane = jnp.arange(LANE)
    par, comp, freq = lane // 64, (lane // 32) % 2, lane % 32
    within = comp * ATT_DH + 2 * freq + par
    return (jnp.arange(ATT_HEADS)[:, None] * LANE + within[None, :]).reshape(-1), (2 * freq + par)


def _rope_tables(l, tm):
    n_freq = ATT_DH // 4
    rows = l // GRID_W
    inv = ROPE_BASE ** (-jnp.arange(n_freq, dtype=f32) / n_freq)
    row = jnp.repeat(jnp.arange(rows, dtype=f32), GRID_W)
    col = jnp.tile(jnp.arange(GRID_W, dtype=f32), rows)
    ang = jnp.concatenate([row[:, None] * inv, col[:, None] * inv], axis=-1)
    cos, sin = jnp.cos(ang), jnp.sin(ang)
    cos_l = jnp.tile(cos, (1, LANE // (ATT_DH // 2)))
    sin_l = jnp.concatenate([-sin, -sin, sin, sin], axis=-1)
    cos_tab = jnp.concatenate([cos_l, jnp.ones((tm, LANE), f32)], axis=0)
    sin_tab = jnp.concatenate([sin_l, jnp.zeros((tm, LANE), f32)], axis=0)
    return cos_tab, sin_tab


def _tiles(l, lc, nb):
    tm = min(1024, l)
    tfm = min(512, l)
    tp = min(256, l)
    tq = min(256, l)
    for tile in (tm, tfm, tp):
        assert l % tile == 0 and (nb * lc) % tile == 0
    return tm, tfm, tp, tq


def kernel(x, c, ctx, c_ctx, ada_w, ada_b, ffn1_norm, ffn1_w_gu, ffn1_w_down, mix_norm, w_in, q_norm, k_norm,
           lambda_q1, lambda_k1, lambda_q2, lambda_k2, attn_subln, conv_w, conv_b, dt_bias, a_log, d_skip,
           ssm_norm, w_branch_attn, w_branch_ssm, w_out, ffn2_norm, ffn2_w_gu, ffn2_w_down):
    nb, l, d = x.shape
    lc = ctx.shape[1]
    depth = ada_w.shape[0]
    t_lat, t_ctx = nb * l, nb * lc
    t = t_lat + t_ctx
    tm, tfm, tp, tq = _tiles(l, lc, nb)
    tf = 512
    tn = 1024
    assert nb < COND_ROWS and l % GRID_W == 0

    xall = jnp.concatenate([x.reshape(t_lat, d), ctx.reshape(t_ctx, d)], axis=0)
    conds = jnp.zeros((COND_ROWS, d), f32).at[:nb].set(c).at[nb].set(c_ctx)
    mod = _adaln(conds, ada_w, ada_b)
    mod3 = mod.reshape(depth * COND_ROWS, 1, N_MOD * d)

    perm, norm_idx = _qk_layout()
    cos_tab, sin_tab = _rope_tables(l, tm)
    rope_fn = lambda i: jnp.where(i < t_lat // tm, i % (l // tm), l // tm)
    lane_n = jnp.arange(QK_NORM_W)
    grp = (lane_n // LANE) * 2 + (lane_n // 32) % 2
    gmat = (grp[:, None] == grp[None, :]).astype(bf16)

    s0 = 3 * ATT_WIDTH
    dt0 = s0 + SSM_INNER + CONV_DIM
    g0 = dt0 + 2 * SSM_HEADS

    def cond_row(tile_rows, base):
        return lambda ti: base + jnp.where(ti < t_lat // tile_rows, ti // (l // tile_rows), nb)

    for i in range(depth):
        last = i == depth - 1
        row_m = cond_row(tm, i * COND_ROWS)
        row_f = cond_row(tfm, i * COND_ROWS)
        row_p = cond_row(tp, i * COND_ROWS)
        lam_init = 0.8 - 0.6 * math.exp(-0.3 * i)

        w = w_in[i]
        w_p = jnp.concatenate([
            w[:, perm], w[:, ATT_WIDTH + perm], w[:, 2 * ATT_WIDTH:s0],
            w[:, s0 + SSM_INNER:dt0], w[:, s0:s0 + SSM_INNER], w[:, g0:]], axis=1).astype(bf16)
        zpad = jnp.zeros((d, LANE - SSM_HEADS), f32)
        w_dt = jnp.concatenate([w[:, dt0:dt0 + SSM_HEADS], zpad, w[:, dt0 + SSM_HEADS:g0], zpad],
                               axis=1).astype(bf16)
        qkw = jnp.stack([jnp.tile(q_norm[i][norm_idx] * (LOG2E * ATT_DH ** -0.5), ATT_HEADS),
                         jnp.tile(k_norm[i][norm_idx], ATT_HEADS)]).reshape(2, 1, ATT_WIDTH)
        w_gu1, w_d1 = ffn1_w_gu[i].astype(bf16), ffn1_w_down[i].astype(bf16)
        w_gu2, w_d2 = ffn2_w_gu[i].astype(bf16), ffn2_w_down[i].astype(bf16)
        w_ba, w_bs, w_o = w_branch_attn[i].astype(bf16), w_branch_ssm[i].astype(bf16), w_out[i].astype(bf16)
        lam_params = (lambda_q1[i], lambda_k1[i], lambda_q2[i], lambda_k2[i])
        d_skip_lanes = jnp.repeat(d_skip[i], SSM_HEADDIM).reshape(1, SSM_INNER)

        xall = _ffn(xall, t, mod3, row_f, 0, ffn1_norm[i], w_gu1, w_d1, tfm, tf)

        p, dt_raw = _inproj(xall, mod3, row_m, rope_fn, mix_norm[i], w_p, w_dt, cos_tab, sin_tab, gmat, qkw,
                            tm, tn)
        o_attn = _attention(p, t_lat, 0, l, tq, [(0, l), (t_lat, lc)], lam_init, lam_params, attn_subln[i])
        xbc = _conv(p, conv_w[i], conv_b[i], t_lat, l, lc)
        y2 = _ssd(xbc, dt_raw, dt_bias[i], a_log[i], nb, t_lat, l, lc)
        if last:
            n_rows, o_attn_c = t_lat, o_attn
        else:
            n_rows = t
            o_attn_c = _attention(p, t_ctx, t_lat, lc, min(tq, lc), [(t_lat, lc)], lam_init, lam_params,
                                  attn_subln[i])
        xall = _post(o_attn, o_attn_c, t_lat // tp, y2, xbc, p, xall, n_rows, mod3, row_p, d_skip_lanes,
                     ssm_norm[i], w_ba, w_bs, w_o, tp)

        xall = _ffn(xall, n_rows, mod3, row_f, 2, ffn2_norm[i], w_gu2, w_d2, tfm, tf)

    return xall[:t_lat].reshape(nb, l, d)
```
